```python
import jax
import jax.numpy as jnp
from jax import lax
import numpy as np

D_MODEL = 2048
BATCH = 1
SEQ = 16384
DEPTH = 2

GRID_W = 64
CTX_LEN = 256
NORM_EPS = 1e-6
MLA_HEADS = D_MODEL // 256
MLA_Q_RANK = D_MODEL // 4
MLA_KV_RANK = D_MODEL // 8
MLA_NOPE = 128
MLA_ROPE = 64
MLA_V = 128
MLA_SCALE = (MLA_NOPE + MLA_ROPE) ** -0.5
ROPE_THETA = 10000.0
Q_BLOCK = 128
NA_HEADS = D_MODEL // 256
NA_DIM = 128
NA_WIN_H = 8
NA_WIN_W = 16
NA_SCALE = NA_DIM ** -0.5
NA_WIDTH = NA_HEADS * NA_DIM
FNET_GROUPS = 8
FNET_GROUP_DIM = D_MODEL // FNET_GROUPS
N_EXPERTS = 16
EXPERT_FF = D_MODEL
EC_CAPACITY = 2
IN_SPLITS = (MLA_Q_RANK,
             MLA_Q_RANK + MLA_KV_RANK,
             MLA_Q_RANK + MLA_KV_RANK + MLA_ROPE,
             MLA_Q_RANK + MLA_KV_RANK + MLA_ROPE + NA_WIDTH,
             MLA_Q_RANK + MLA_KV_RANK + MLA_ROPE + 2 * NA_WIDTH)
IN_WIDTH = MLA_Q_RANK + MLA_KV_RANK + MLA_ROPE + 3 * NA_WIDTH
MIX_WIDTH = MLA_HEADS * MLA_V + NA_WIDTH
N_ATTN_LAYERS = (DEPTH + 1) // 2
N_FOURIER_LAYERS = DEPTH // 2

kernel_name = 'hybrid_mla_natten_fnet_ecmoe_dit'


def rms_norm(x, g):
    xf = x.astype(jnp.float32)
    y = xf * lax.rsqrt(jnp.mean(xf * xf, axis=-1, keepdims=True) + NORM_EPS)
    return (y * g.astype(jnp.float32)).astype(x.dtype)


def modulate(x, g, shift, scale):
    return rms_norm(x, g) * (1 + scale) + shift


def ada_terms(cvec, w, b):
    m = jax.nn.silu(cvec) @ w + b
    return jnp.split(m[..., None, :], 6, axis=-1)


def axial_angles(n):
    t = jnp.arange(n)
    n_freq = MLA_ROPE // 4
    inv_freq = ROPE_THETA ** (-jnp.arange(n_freq, dtype=jnp.float32) / n_freq)
    row = (t // GRID_W).astype(jnp.float32)[:, None] * inv_freq
    col = (t % GRID_W).astype(jnp.float32)[:, None] * inv_freq
    return row, col


def rotate_pairs(x, ang):
    x1, x2 = jnp.split(x, 2, axis=-1)
    cos = jnp.cos(ang).astype(x.dtype)
    sin = jnp.sin(ang).astype(x.dtype)
    return jnp.concatenate([x1 * cos - x2 * sin, x2 * cos + x1 * sin], axis=-1)


def rope_2d(x, ang_row, ang_col):
    x_row, x_col = jnp.split(x, 2, axis=-1)
    return jnp.concatenate([rotate_pairs(x_row, ang_row), rotate_pairs(x_col, ang_col)], axis=-1)


def project_heads(h, w_in, g_qn, w_uq, g_kvn, w_ukv, with_q):
    lead = h.shape[:-1]
    c_q, c_kv, k_rope, q_na, k_na, v_na = jnp.split(h @ w_in, IN_SPLITS, axis=-1)
    kv = (rms_norm(c_kv, g_kvn) @ w_ukv).reshape(*lead, MLA_HEADS, MLA_NOPE + MLA_V)
    heads = lambda t: t.reshape(*lead, NA_HEADS, NA_DIM)
    keys = (kv[..., :MLA_NOPE], k_rope, kv[..., MLA_NOPE:], heads(k_na), heads(v_na))
    if not with_q:
        return None, keys
    q = (rms_norm(c_q, g_qn) @ w_uq).reshape(*lead, MLA_HEADS, MLA_NOPE + MLA_ROPE)
    return (q[..., :MLA_NOPE], q[..., MLA_NOPE:], heads(q_na)), keys


def mla_attend(q_nope, q_rope, k_nope, k_rope, v):
    s = (jnp.einsum('bqhd,bkhd->bhqk', q_nope, k_nope).astype(jnp.float32)
         + jnp.einsum('bqhr,bkr->bhqk', q_rope, k_rope).astype(jnp.float32)) * MLA_SCALE
    p = jax.nn.softmax(s, axis=-1).astype(v.dtype)
    return jnp.einsum('bhqk,bkhd->bqhd', p, v)


def dense_attend(q, k, v, scale):
    s = jnp.einsum('bqhd,bkhd->bhqk', q, k).astype(jnp.float32) * scale
    p = jax.nn.softmax(s, axis=-1).astype(v.dtype)
    return jnp.einsum('bhqk,bkhd->bqhd', p, v)


def na_latent(q, k, v, k_ctx, v_ctx, rel_bias):
    b, n, nh, dh = q.shape
    rows_total = n // GRID_W
    wh = min(NA_WIN_H, rows_total)
    r = jnp.arange(rows_total)
    row_start = jnp.clip(r - wh // 2, 0, rows_total - wh)
    rows = row_start[:, None] + jnp.arange(wh)
    cq = jnp.arange(GRID_W)
    col_start = jnp.clip(cq - NA_WIN_W // 2, 0, GRID_W - NA_WIN_W)
    ck = jnp.arange(GRID_W)
    valid = (ck[None, :] >= col_start[:, None]) & (ck[None, :] < col_start[:, None] + NA_WIN_W)
    ro = rows - r[:, None] + NA_WIN_H - 1
    co = jnp.clip(ck[None, :] - cq[:, None] + NA_WIN_W - 1, 0, 2 * NA_WIN_W - 2)
    bias = rel_bias[:, ro[:, None, :, None], co[None, :, None, :]]
    qg = q.reshape(b, rows_total, GRID_W, nh, dh)
    kb = k.reshape(b, rows_total, GRID_W, nh, dh)[:, rows]
    vb = v.reshape(b, rows_total, GRID_W, nh, dh)[:, rows]
    s_loc = jnp.einsum('brqhd,brjkhd->bhrqjk', qg, kb).astype(jnp.float32) * NA_SCALE + bias[None].astype(jnp.float32)
    s_loc = jnp.where(valid[:, None, :], s_loc, jnp.finfo(jnp.float32).min)
    s_ctx = jnp.einsum('bnhd,bchd->bhnc', q, k_ctx).astype(jnp.float32) * NA_SCALE
    n_ctx = k_ctx.shape[1]
    s = jnp.concatenate([s_ctx, s_loc.reshape(b, nh, n, wh * GRID_W)], axis=-1)
    p = jax.nn.softmax(s, axis=-1).astype(v.dtype)
    p_ctx = p[..., :n_ctx]
    p_loc = p[..., n_ctx:].reshape(b, nh, rows_total, GRID_W, wh, GRID_W)
    o_loc = jnp.einsum('bhrqjk,brjkhd->brqhd', p_loc, vb).reshape(b, n, nh, dh)
    return jnp.einsum('bhnc,bchd->bnhd', p_ctx, v_ctx) + o_loc


def attention_mixer(h, hc, ang_row, ang_col, w_in, g_qn, w_uq, g_kvn, w_ukv, rel_bias, w_out, ctx_live):
    b, n, _ = h.shape
    (qn, qr, qa), (kn, kr, v, ka, va) = project_heads(h, w_in, g_qn, w_uq, g_kvn, w_ukv, True)
    qr = rope_2d(qr, ang_row[:, None], ang_col[:, None])
    kr = rope_2d(kr, ang_row, ang_col)
    cq, (ckn, ckr, cv, cka, cva) = project_heads(hc, w_in, g_qn, w_uq, g_kvn, w_ukv, ctx_live)
    kn_all = jnp.concatenate([ckn, kn], axis=1)
    kr_all = jnp.concatenate([ckr, kr], axis=1)
    v_all = jnp.concatenate([cv, v], axis=1)
    nb = n // Q_BLOCK
    def blocks(t):
        return jnp.moveaxis(t.reshape(b, nb, Q_BLOCK, *t.shape[2:]), 1, 0)
    o_a = lax.map(lambda qb: mla_attend(qb[0], qb[1], kn_all, kr_all, v_all), (blocks(qn), blocks(qr)))
    o_a = jnp.moveaxis(o_a, 0, 1).reshape(b, n, MLA_HEADS * MLA_V)
    o_b = na_latent(qa, ka, va, cka, cva, rel_bias).reshape(b, n, NA_WIDTH)
    y = jnp.concatenate([o_a, o_b], axis=-1) @ w_out
    if not ctx_live:
        return y, None
    cqn, cqr, cqa = cq
    n_ctx = hc.shape[1]
    oc_a = mla_attend(cqn, cqr, ckn, ckr, cv).reshape(b, n_ctx, MLA_HEADS * MLA_V)
    oc_b = dense_attend(cqa, cka, cva, NA_SCALE).reshape(b, n_ctx, NA_WIDTH)
    return y, jnp.concatenate([oc_a, oc_b], axis=-1) @ w_out


def fourier_mixer(h, w_out):
    b, n, d = h.shape
    hg = h.astype(jnp.float32).reshape(b, n, FNET_GROUPS, FNET_GROUP_DIM)
    f = jnp.fft.fft2(hg, axes=(1, 3), norm='ortho').real
    return f.reshape(b, n, d).astype(h.dtype) @ w_out


def ec_moe(h, w_router, w_gate, w_up, w_down):
    b, n, d = h.shape
    cap = EC_CAPACITY * n // N_EXPERTS
    aff = jax.nn.softmax((h @ w_router).astype(jnp.float32), axis=-1)
    gates, idx = lax.top_k(jnp.swapaxes(aff, 1, 2), cap)
    def per_sample(hs, gs, ids):
        xs = hs[ids]
        a = jnp.einsum('ecd,edf->ecf', xs, w_gate)
        u = jnp.einsum('ecd,edf->ecf', xs, w_up)
        y = jnp.einsum('ecf,efd->ecd', jax.nn.silu(a) * u, w_down) * gs[..., None].astype(hs.dtype)
        return jnp.zeros_like(hs).at[ids.reshape(-1)].add(y.reshape(-1, d))
    return jax.vmap(per_sample)(h, gates, idx)


def setup_inputs(seed: int = 0) -> dict:
    key = jax.random.key(seed)
    ks = jax.random.split(key, 22)
    def nrm(k, shape, s):
        return jax.random.normal(k, shape, jnp.float32) * s
    d = D_MODEL
    na_l = N_ATTN_LAYERS
    nf_l = N_FOURIER_LAYERS
    return {
        'x': nrm(ks[0], (BATCH, SEQ, d), 1.0),
        'c': nrm(ks[1], (BATCH, d), 1.0),
        'ctx': nrm(ks[2], (BATCH, CTX_LEN, d), 1.0),
        'c_ctx': nrm(ks[3], (d,), 1.0),
        'ada_w': nrm(ks[4], (DEPTH, d, 6 * d), 0.5 * d ** -0.5),
        'ada_b': nrm(ks[5], (DEPTH, 6 * d), 0.02),
        'g_mix': 1.0 + nrm(ks[6], (DEPTH, d), 0.02),
        'g_ffn': 1.0 + nrm(ks[7], (DEPTH, d), 0.02),
        'g_final': 1.0 + nrm(ks[8], (d,), 0.02),
        'w_in': nrm(ks[9], (na_l, d, IN_WIDTH), d ** -0.5),
        'g_qn': 1.0 + nrm(ks[10], (na_l, MLA_Q_RANK), 0.02),
        'w_uq': nrm(ks[11], (na_l, MLA_Q_RANK, MLA_HEADS * (MLA_NOPE + MLA_ROPE)), MLA_Q_RANK ** -0.5),
        'g_kvn': 1.0 + nrm(ks[12], (na_l, MLA_KV_RANK), 0.02),
        'w_ukv': nrm(ks[13], (na_l, MLA_KV_RANK, MLA_HEADS * (MLA_NOPE + MLA_V)), MLA_KV_RANK ** -0.5),
        'na_rel_bias': nrm(ks[14], (na_l, NA_HEADS, 2 * NA_WIN_H - 1, 2 * NA_WIN_W - 1), 0.1),
        'w_out_attn': nrm(ks[15], (na_l, MIX_WIDTH, d), MIX_WIDTH ** -0.5),
        'w_out_fourier': nrm(ks[16], (nf_l, d, d), d ** -0.5),
        'w_router': nrm(ks[17], (DEPTH, d, N_EXPERTS), d ** -0.5),
        'w_gate': nrm(ks[18], (DEPTH, N_EXPERTS, d, EXPERT_FF), d ** -0.5),
        'w_up': nrm(ks[19], (DEPTH, N_EXPERTS, d, EXPERT_FF), d ** -0.5),
        'w_down': nrm(ks[20], (DEPTH, N_EXPERTS, EXPERT_FF, d), EXPERT_FF ** -0.5),
    }


def reference(x, c, ctx, c_ctx, ada_w, ada_b, g_mix, g_ffn, g_final, w_in, g_qn, w_uq, g_kvn, w_ukv,
              na_rel_bias, w_out_attn, w_out_fourier, w_router, w_gate, w_up, w_down):
    n_tok = x.shape[1]
    ang_row, ang_col = axial_angles(n_tok)
    last_attn = 2 * ((DEPTH - 1) // 2)
    for i in range(DEPTH):
        ctx_live = i < last_attn
        sh_m, sc_m, g_m, sh_f, sc_f, g_f = ada_terms(c, ada_w[i], ada_b[i])
        h = modulate(x, g_mix[i], sh_m, sc_m)
        if i % 2 == 0 or ctx_live:
            csh_m, csc_m, cg_m, csh_f, csc_f, cg_f = ada_terms(c_ctx, ada_w[i], ada_b[i])
            hc = modulate(ctx, g_mix[i], csh_m, csc_m)
        j = i // 2
        if i % 2 == 0:
            y, yc = attention_mixer(h, hc, ang_row, ang_col, w_in[j], g_qn[j], w_uq[j], g_kvn[j], w_ukv[j],
                                    na_rel_bias[j], w_out_attn[j], ctx_live)
        else:
            y = fourier_mixer(h, w_out_fourier[j])
            yc = fourier_mixer(hc, w_out_fourier[j]) if ctx_live else None
        x = x + g_m * y
        x = x + g_f * ec_moe(modulate(x, g_ffn[i], sh_f, sc_f), w_router[i], w_gate[i], w_up[i], w_down[i])
        if ctx_live:
            ctx = ctx + cg_m * yc
            ctx = ctx + cg_f * ec_moe(modulate(ctx, g_ffn[i], csh_f, csc_f), w_router[i], w_gate[i], w_up[i], w_down[i])
    return rms_norm(x, g_final)
```

```python
import functools
import math

import jax
import jax.numpy as jnp
from jax import lax
from jax.experimental import pallas as pl
from jax.experimental.pallas import tpu as pltpu

F32 = jnp.float32
BF16 = jnp.bfloat16

GRID_W = 64
NORM_EPS = 1e-6
MLA_NOPE = 128
MLA_ROPE = 64
MLA_V = 128
MLA_QK_PAD = 256
ROPE_THETA = 10000.0
NA_DIM = 128
NA_WIN_H = 8
NA_WIN_W = 16
NA_ROW_BLOCK = 8
NA_KEY_ROWS = 16
FNET_GROUPS = 8
N_EXPERTS = 16
EC_CAPACITY = 2
ROUTER_LANES = 128
NEG_BIG = -1e30

V7X_VMEM_LIMIT_BYTES = 56 * 1024 * 1024


def _cparams(sem):
    return pltpu.CompilerParams(dimension_semantics=sem, vmem_limit_bytes=V7X_VMEM_LIMIT_BYTES)


def _pick(n, candidates):
    for c in candidates:
        if n % c == 0:
            return c
    return n


def _ada_kernel(c_ref, w_ref, b_ref, o_ref):
    c = c_ref[...]
    s = c / (1.0 + jnp.exp(-c))
    o_ref[...] = jnp.dot(s, w_ref[...], preferred_element_type=F32,
                         precision=lax.Precision.HIGHEST) + b_ref[...]


def _ada_terms(cvecs, w, b):
    d, n6 = w.shape
    tn = _pick(n6, (1024, 512, 256, 128))
    return pl.pallas_call(
        _ada_kernel,
        grid=(n6 // tn,),
        in_specs=[pl.BlockSpec((8, d), lambda j: (0, 0)),
                  pl.BlockSpec((d, tn), lambda j: (0, j)),
                  pl.BlockSpec((1, tn), lambda j: (0, j))],
        out_specs=pl.BlockSpec((8, tn), lambda j: (0, j)),
        out_shape=jax.ShapeDtypeStruct((8, n6), F32),
        compiler_params=_cparams(("parallel",)),
        name="ada_terms",
    )(cvecs, w, b[None])


def _modulated(x, g, shift, scale):
    ms = jnp.mean(x * x, axis=-1, keepdims=True)
    y = x * lax.rsqrt(ms + NORM_EPS) * g
    return y * (1.0 + scale) + shift


def _mod_kernel(x_ref, g_ref, sh_ref, sc_ref, h_ref):
    h_ref[...] = _modulated(x_ref[...], g_ref[...], sh_ref[...], sc_ref[...]).astype(h_ref.dtype)


def _modulate(x, g, shift, scale, out_dtype=BF16):
    n, d = x.shape
    tm = _pick(n, (512, 256, 128, 64, 8))
    row = pl.BlockSpec((tm, d), lambda i: (i, 0))
    vec = pl.BlockSpec((1, d), lambda i: (0, 0))
    return pl.pallas_call(
        _mod_kernel,
        grid=(n // tm,),
        in_specs=[row, vec, vec, vec],
        out_specs=row,
        out_shape=jax.ShapeDtypeStruct((n, d), out_dtype),
        compiler_params=_cparams(("parallel",)),
        name="modulate",
    )(x, g, shift, scale)


def _router_kernel(n_experts, x_ref, g_ref, sh_ref, sc_ref, wr_ref, h_ref, aff_ref):
    h = _modulated(x_ref[...], g_ref[...], sh_ref[...], sc_ref[...])
    h_ref[...] = h.astype(h_ref.dtype)
    logits = jnp.dot(h, wr_ref[...], preferred_element_type=F32, precision=lax.Precision.HIGHEST)
    lane = lax.broadcasted_iota(jnp.int32, logits.shape, 1)
    logits = jnp.where(lane < n_experts, logits, NEG_BIG)
    m = jnp.max(logits, axis=-1, keepdims=True)
    p = jnp.exp(logits - m)
    aff_ref[...] = p / jnp.sum(p, axis=-1, keepdims=True)


def _modulate_router(x, g, shift, scale, w_router):
    n, d = x.shape
    n_experts = w_router.shape[1]
    wr = jnp.zeros((d, ROUTER_LANES), F32).at[:, :n_experts].set(w_router)
    tm = _pick(n, (512, 256, 128, 64, 8))
    row = pl.BlockSpec((tm, d), lambda i: (i, 0))
    vec = pl.BlockSpec((1, d), lambda i: (0, 0))
    return pl.pallas_call(
        functools.partial(_router_kernel, n_experts),
        grid=(n // tm,),
        in_specs=[row, vec, vec, vec, pl.BlockSpec((d, ROUTER_LANES), lambda i: (0, 0))],
        out_specs=[row, pl.BlockSpec((tm, ROUTER_LANES), lambda i: (i, 0))],
        out_shape=[jax.ShapeDtypeStruct((n, d), BF16),
                   jax.ShapeDtypeStruct((n, ROUTER_LANES), F32)],
        compiler_params=_cparams(("parallel",)),
        name="modulate_router",
    )(x, g, shift, scale, wr)


def _resid_kernel(x_ref, gate_ref, y_ref, o_ref):
    o_ref[...] = x_ref[...] + gate_ref[...] * y_ref[...]


def _residual(x, gate, y):
    n, d = x.shape
    tm = _pick(n, (512, 256, 128, 64, 8))
    row = pl.BlockSpec((tm, d), lambda i: (i, 0))
    vec = pl.BlockSpec((1, d), lambda i: (0, 0))
    return pl.pallas_call(
        _resid_kernel,
        grid=(n // tm,),
        in_specs=[row, vec, row],
        out_specs=row,
        out_shape=jax.ShapeDtypeStruct((n, d), F32),
        compiler_params=_cparams(("parallel",)),
        name="residual",
    )(x, gate, y)


def _final_kernel(x_ref, gate_ref, y_ref, g_ref, o_ref):
    x = x_ref[...] + gate_ref[...] * y_ref[...]
    ms = jnp.mean(x * x, axis=-1, keepdims=True)
    o_ref[...] = x * lax.rsqrt(ms + NORM_EPS) * g_ref[...]


def _final_norm(x, gate, y, g):
    n, d = x.shape
    tm = _pick(n, (512, 256, 128, 64, 8))
    row = pl.BlockSpec((tm, d), lambda i: (i, 0))
    vec = pl.BlockSpec((1, d), lambda i: (0, 0))
    return pl.pallas_call(
        _final_kernel,
        grid=(n // tm,),
        in_specs=[row, vec, row, vec],
        out_specs=row,
        out_shape=jax.ShapeDtypeStruct((n, d), F32),
        compiler_params=_cparams(("parallel",)),
        name="final_norm",
    )(x, gate, y, g)


def _mm_kernel(a_ref, w_ref, cs_ref, o_ref):
    acc = jnp.dot(a_ref[...], w_ref[...], preferred_element_type=F32)
    o_ref[...] = (acc * cs_ref[...]).astype(o_ref.dtype)


def _mm_resid_kernel(a_ref, w_ref, x_ref, gate_ref, o_ref):
    acc = jnp.dot(a_ref[...], w_ref[...], preferred_element_type=F32)
    o_ref[...] = x_ref[...] + gate_ref[...] * acc


def _mm_tiles(m, n):
    tm = _pick(m, (1280, 1024, 512, 256, 128, 64, 8))
    tn = n if n <= 1024 else _pick(n, (1024, 512, 256, 128))
    return tm, tn


def _matmul(a, w, col_scale, out_dtype):
    m, k = a.shape
    n = w.shape[1]
    tm, tn = _mm_tiles(m, n)
    return pl.pallas_call(
        _mm_kernel,
        grid=(m // tm, n // tn),
        in_specs=[pl.BlockSpec((tm, k), lambda i, j: (i, 0)),
                  pl.BlockSpec((k, tn), lambda i, j: (0, j)),
                  pl.BlockSpec((1, tn), lambda i, j: (0, j))],
        out_specs=pl.BlockSpec((tm, tn), lambda i, j: (i, j)),
        out_shape=jax.ShapeDtypeStruct((m, n), out_dtype),
        compiler_params=_cparams(("parallel", "parallel")),
        name="matmul",
    )(a, w, col_scale)


def _matmul_residual(a, w, x, gate):
    m, k = a.shape
    n = w.shape[1]
    tm, tn = _mm_tiles(m, n)
    return pl.pallas_call(
        _mm_resid_kernel,
        grid=(m // tm, n // tn),
        in_specs=[pl.BlockSpec((tm, k), lambda i, j: (i, 0)),
                  pl.BlockSpec((k, tn), lambda i, j: (0, j)),
                  pl.BlockSpec((tm, tn), lambda i, j: (i, j)),
                  pl.BlockSpec((1, tn), lambda i, j: (0, j))],
        out_specs=pl.BlockSpec((tm, tn), lambda i, j: (i, j)),
        out_shape=jax.ShapeDtypeStruct((m, n), F32),
        compiler_params=_cparams(("parallel", "parallel")),
        name="matmul_residual",
    )(a, w, x, gate)


def _rms(x, g):
    ms = jnp.mean(x * x, axis=-1, keepdims=True)
    return x * lax.rsqrt(ms + NORM_EPS) * g


def _rope_half(pair, cs):
    t = pair * cs
    t = t + pltpu.roll(t, MLA_ROPE, 1)
    lane = lax.broadcasted_iota(jnp.int32, t.shape, 1)
    return jnp.where(lane < MLA_ROPE, t, 0.0)


def _qprep_kernel(heads, q_rank, scale, a_ref, g_ref, w_ref, cs_ref, q_ref):
    cq = _rms(a_ref[:, :q_rank], g_ref[...]).astype(BF16)
    cs = cs_ref[...]
    for h in range(heads):
        qh = jnp.dot(cq, w_ref[h], preferred_element_type=F32)
        q_ref[h, :, :MLA_NOPE] = (qh[:, :MLA_NOPE] * scale).astype(BF16)
        q_ref[h, :, MLA_NOPE:] = (_rope_half(qh[:, MLA_NOPE:], cs) * scale).astype(BF16)


def _kvprep_kernel(heads, q_rank, kv_rank, a_ref, g_ref, w_ref, cs_ref, k_ref, v_ref):
    ckv = _rms(a_ref[:, q_rank:q_rank + kv_rank], g_ref[...]).astype(BF16)
    kr = _rope_half(a_ref[:, q_rank + kv_rank:], cs_ref[...]).astype(BF16)
    for h in range(heads):
        kvh = jnp.dot(ckv, w_ref[h], preferred_element_type=F32)
        k_ref[h, :, :MLA_NOPE] = kvh[:, :MLA_NOPE].astype(BF16)
        k_ref[h, :, MLA_NOPE:] = kr
        v_ref[h] = kvh[:, MLA_NOPE:].astype(BF16)


def _qprep(a, row_off_blocks, n_rows, g_qn, wq, cs, heads, q_rank, scale, tm):
    aw = a.shape[1]
    return pl.pallas_call(
        functools.partial(_qprep_kernel, heads, q_rank, scale),
        grid=(n_rows // tm,),
        in_specs=[pl.BlockSpec((tm, aw), lambda i: (i + row_off_blocks, 0)),
                  pl.BlockSpec((1, q_rank), lambda i: (0, 0)),
                  pl.BlockSpec((heads, q_rank, MLA_QK_PAD), lambda i: (0, 0, 0)),
                  pl.BlockSpec((tm, 2 * MLA_ROPE), lambda i: (i + row_off_blocks, 0))],
        out_specs=pl.BlockSpec((heads, tm, MLA_QK_PAD), lambda i: (0, i, 0)),
        out_shape=jax.ShapeDtypeStruct((heads, n_rows, MLA_QK_PAD), BF16),
        compiler_params=_cparams(("parallel",)),
        name="mla_q_prep",
    )(a, g_qn, wq, cs)


def _kvprep(a, g_kvn, wkv, cs, heads, q_rank, kv_rank, tm):
    n_rows, aw = a.shape
    return pl.pallas_call(
        functools.partial(_kvprep_kernel, heads, q_rank, kv_rank),
        grid=(n_rows // tm,),
        in_specs=[pl.BlockSpec((tm, aw), lambda i: (i, 0)),
                  pl.BlockSpec((1, kv_rank), lambda i: (0, 0)),
                  pl.BlockSpec((heads, kv_rank, MLA_NOPE + MLA_V), lambda i: (0, 0, 0)),
                  pl.BlockSpec((tm, 2 * MLA_ROPE), lambda i: (i, 0))],
        out_specs=[pl.BlockSpec((heads, tm, MLA_QK_PAD), lambda i: (0, i, 0)),
                   pl.BlockSpec((heads, tm, MLA_V), lambda i: (0, i, 0))],
        out_shape=[jax.ShapeDtypeStruct((heads, n_rows, MLA_QK_PAD), BF16),
                   jax.ShapeDtypeStruct((heads, n_rows, MLA_V), BF16)],
        compiler_params=_cparams(("parallel",)),
        name="mla_kv_prep",
    )(a, g_kvn, wkv, cs)


def _dot_nt(a, b):
    return lax.dot_general(a, b, (((1,), (1,)), ((), ())), preferred_element_type=F32)


def _mla_kernel(tk, n_kv, q_ref, k_ref, v_ref, o_ref):
    q = q_ref[0]
    tq = q.shape[0]

    def body(j, carry):
        m, l, acc = carry
        start = pl.multiple_of(j * tk, tk)
        s = _dot_nt(q, k_ref[0, pl.ds(start, tk), :])
        m_new = jnp.maximum(m, jnp.max(s, axis=-1, keepdims=True))
        p = jnp.exp(s - m_new)
        alpha = jnp.exp(m - m_new)
        l = alpha * l + jnp.sum(p, axis=-1, keepdims=True)
        acc = alpha * acc + jnp.dot(p.astype(BF16), v_ref[0, pl.ds(start, tk), :],
                                    preferred_element_type=F32)
        return m_new, l, acc

    m0 = jnp.full((tq, 1), NEG_BIG, F32)
    l0 = jnp.zeros((tq, 1), F32)
    acc0 = jnp.zeros((tq, MLA_V), F32)
    _, l, acc = lax.fori_loop(0, n_kv // tk, body, (m0, l0, acc0))
    o_ref[...] = (acc / l).astype(o_ref.dtype)


def _mla_attention(q, k, v):
    heads, n, _ = q.shape
    n_kv = k.shape[1]
    tq = _pick(n, (512, 256, 128))
    tk = _pick(n_kv, (640, 1280, 512, 256, 128))
    return pl.pallas_call(
        functools.partial(_mla_kernel, tk, n_kv),
        grid=(heads, n // tq),
        in_specs=[pl.BlockSpec((1, tq, MLA_QK_PAD), lambda h, i: (h, i, 0)),
                  pl.BlockSpec((1, n_kv, MLA_QK_PAD), lambda h, i: (h, 0, 0)),
                  pl.BlockSpec((1, n_kv, MLA_V), lambda h, i: (h, 0, 0))],
        out_specs=pl.BlockSpec((tq, MLA_V), lambda h, i: (i, h)),
        out_shape=jax.ShapeDtypeStruct((n, heads * MLA_V), BF16),
        compiler_params=_cparams(("parallel", "parallel")),
        name="mla_attention",
    )(q, k, v)


def _na_kernel(n_ctx, rows_total, win_h, key_rows, q_ref, k_ref, v_ref, t_ref, o_ref, bm_ref):
    b = pl.program_id(1)
    nb = pl.num_programs(1)
    r0 = b * NA_ROW_BLOCK
    base = jnp.clip(r0 - NA_WIN_H // 2, 0, rows_total - key_rows)

    @pl.when((b <= 1) | (b == nb - 1))
    def _():
        for i in range(NA_ROW_BLOCK):
            rs = jnp.clip(r0 + i - win_h // 2, 0, rows_total - win_h)
            for j in range(key_rows):
                kr = base + j
                valid = (kr >= rs) & (kr < rs + win_h)
                idx = jnp.clip(kr - (r0 + i) + NA_WIN_H - 1, 0, 2 * NA_WIN_H - 2)
                bm_ref[i * GRID_W:(i + 1) * GRID_W, j * GRID_W:(j + 1) * GRID_W] = jnp.where(
                    valid, t_ref[0, idx], NEG_BIG)

    q = q_ref[...]
    n_loc = key_rows * GRID_W
    loc0 = pl.multiple_of(n_ctx + base * GRID_W, GRID_W)
    s_ctx = _dot_nt(q, k_ref[0:n_ctx, :])
    s_loc = _dot_nt(q, k_ref[pl.ds(loc0, n_loc), :]) + bm_ref[...]
    m = jnp.maximum(jnp.max(s_ctx, axis=-1, keepdims=True), jnp.max(s_loc, axis=-1, keepdims=True))
    p_ctx = jnp.exp(s_ctx - m)
    p_loc = jnp.exp(s_loc - m)
    l = jnp.sum(p_ctx, axis=-1, keepdims=True) + jnp.sum(p_loc, axis=-1, keepdims=True)
    o = jnp.dot(p_ctx.astype(BF16), v_ref[0:n_ctx, :], preferred_element_type=F32)
    o = o + jnp.dot(p_loc.astype(BF16), v_ref[pl.ds(loc0, n_loc), :], preferred_element_type=F32)
    o_ref[...] = (o / l).astype(o_ref.dtype)


def _na_bias_table(rel_bias):
    cq = jnp.arange(GRID_W)
    ck = jnp.arange(GRID_W)
    col_start = jnp.clip(cq - NA_WIN_W // 2, 0, GRID_W - NA_WIN_W)
    valid = (ck[None, :] >= col_start[:, None]) & (ck[None, :] < col_start[:, None] + NA_WIN_W)
    co = jnp.clip(ck[None, :] - cq[:, None] + NA_WIN_W - 1, 0, 2 * NA_WIN_W - 2)
    onehot = (co[None] == jnp.arange(2 * NA_WIN_W - 1)[:, None, None]).astype(F32)
    t = jnp.einsum('hrc,cqk->hrqk', rel_bias, onehot, precision=lax.Precision.HIGHEST)
    return jnp.where(valid[None, None], t, NEG_BIG)


def _na_attention(q, kv, table, heads, n_ctx):
    n = q.shape[0]
    n_kv = kv.shape[0]
    rows_total = n // GRID_W
    win_h = min(NA_WIN_H, rows_total)
    key_rows = min(NA_KEY_ROWS, rows_total)
    tq = NA_ROW_BLOCK * GRID_W
    n_off = table.shape[1]
    return pl.pallas_call(
        functools.partial(_na_kernel, n_ctx, rows_total, win_h, key_rows),
        grid=(heads, n // tq),
        in_specs=[pl.BlockSpec((tq, NA_DIM), lambda h, i: (i, h)),
                  pl.BlockSpec((n_kv, NA_DIM), lambda h, i: (0, h)),
                  pl.BlockSpec((n_kv, NA_DIM), lambda h, i: (0, heads + h)),
                  pl.BlockSpec((1, n_off, GRID_W, GRID_W), lambda h, i: (h, 0, 0, 0))],
        out_specs=pl.BlockSpec((tq, NA_DIM), lambda h, i: (i, h)),
        out_shape=jax.ShapeDtypeStruct((n, heads * NA_DIM), BF16),
        scratch_shapes=[pltpu.VMEM((tq, key_rows * GRID_W), F32)],
        compiler_params=_cparams(("parallel", "arbitrary")),
        name="na_attention",
    )(q, kv, kv, table)


def _fft_a_kernel(groups, x_ref, gate_ref, y_ref, g_ref, sh_ref, sc_ref, w_ref, xo_ref, zr_ref, zi_ref):
    x = x_ref[...] + gate_ref[...] * y_ref[...]
    xo_ref[...] = x
    h = _modulated(x, g_ref[...], sh_ref[...], sc_ref[...]).astype(BF16)
    dg = h.shape[1] // groups
    for g in range(groups):
        z = jnp.dot(h[:, g * dg:(g + 1) * dg], w_ref[...], preferred_element_type=F32)
        zr_ref[:, g * dg:(g + 1) * dg] = z[:, :dg].astype(BF16)
        zi_ref[:, g * dg:(g + 1) * dg] = z[:, dg:].astype(BF16)


def _fft_b_kernel(zr_ref, zi_ref, w_ref, yr_ref, yi_ref):
    n1 = zr_ref.shape[0]
    z = jnp.concatenate([zr_ref[...], zi_ref[...]], axis=0)
    y = jnp.dot(w_ref[...], z, preferred_element_type=F32)
    yr_ref[...] = y[:n1].astype(BF16)
    yi_ref[...] = y[n1:].astype(BF16)


def _fft_c_kernel(yr_ref, yi_ref, w_ref, o_ref):
    y = jnp.concatenate([yr_ref[0], yi_ref[0]], axis=0)
    o_ref[...] = jnp.dot(w_ref[0], y, preferred_element_type=F32).astype(o_ref.dtype)


def _dft_tables(n, dg):
    lg = int(math.log2(n))
    n1 = 1 << ((lg + 1) // 2)
    n2 = n // n1
    scale_a = 1.0 / math.sqrt(dg)
    sb = 1 << (int(math.log2(n1)) // 2)
    scale_b = 1.0 / sb
    scale_c = 1.0 / (math.sqrt(n) / sb)
    ch = jnp.arange(dg, dtype=jnp.int32)
    ang = (2.0 * math.pi / dg) * ((ch[:, None] * ch[None, :]) % dg).astype(F32)
    wa = jnp.concatenate([jnp.cos(ang), -jnp.sin(ang)], axis=1) * scale_a
    a = jnp.arange(n1, dtype=jnp.int32)
    ang = (2.0 * math.pi / n1) * ((a[:, None] * a[None, :]) % n1).astype(F32)
    c, s = jnp.cos(ang), jnp.sin(ang)
    wb = jnp.concatenate([jnp.concatenate([c, s], axis=1), jnp.concatenate([-s, c], axis=1)], axis=0) * scale_b
    k1 = jnp.arange(n1, dtype=jnp.int32)[:, None, None]
    k2 = jnp.arange(n2, dtype=jnp.int32)[None, :, None]
    b = jnp.arange(n2, dtype=jnp.int32)[None, None, :]
    ang = (2.0 * math.pi / n) * ((b * (k1 + n1 * k2)) % n).astype(F32)
    wc = jnp.concatenate([jnp.cos(ang), jnp.sin(ang)], axis=2) * scale_c
    return n1, n2, wa.astype(BF16), wb.astype(BF16), wc.astype(BF16)


def _fourier_transform(x, gate, y, g, shift, scale):
    n, d = x.shape
    dg = d // FNET_GROUPS
    n1, n2, wa, wb, wc = _dft_tables(n, dg)
    tm = _pick(n, (512, 256, 128, 64, 8))
    row = pl.BlockSpec((tm, d), lambda i: (i, 0))
    vec = pl.BlockSpec((1, d), lambda i: (0, 0))
    x_new, zr, zi = pl.pallas_call(
        functools.partial(_fft_a_kernel, FNET_GROUPS),
        grid=(n // tm,),
        in_specs=[row, vec, row, vec, vec, vec, pl.BlockSpec((dg, 2 * dg), lambda i: (0, 0))],
        out_specs=[row, row, row],
        out_shape=[jax.ShapeDtypeStruct((n, d), F32),
                   jax.ShapeDtypeStruct((n, d), BF16),
                   jax.ShapeDtypeStruct((n, d), BF16)],
        compiler_params=_cparams(("parallel",)),
        name="fft_channels",
    )(x, gate, y, g, shift, scale, wa)

    cols = n2 * d
    tn = _pick(cols, (4096, 2048, 1024, 512, 256, 128))
    col = pl.BlockSpec((n1, tn), lambda j: (0, j))
    yr, yi = pl.pallas_call(
        _fft_b_kernel,
        grid=(cols // tn,),
        in_specs=[col, col, pl.BlockSpec((2 * n1, 2 * n1), lambda j: (0, 0))],
        out_specs=[col, col],
        out_shape=[jax.ShapeDtypeStruct((n1, cols), BF16)] * 2,
        compiler_params=_cparams(("parallel",)),
        name="fft_pos_outer",
    )(zr.reshape(n1, cols), zi.reshape(n1, cols), wb)

    slab = pl.BlockSpec((1, n2, d), lambda k: (k, 0, 0))
    f = pl.pallas_call(
        _fft_c_kernel,
        grid=(n1,),
        in_specs=[slab, slab, pl.BlockSpec((1, n2, 2 * n2), lambda k: (k, 0, 0))],
        out_specs=pl.BlockSpec((n2, d), lambda k: (0, k)),
        out_shape=jax.ShapeDtypeStruct((n2, n1 * d), BF16),
        compiler_params=_cparams(("parallel",)),
        name="fft_pos_inner",
    )(yr.reshape(n1, n2, d), yi.reshape(n1, n2, d), wc)
    return x_new, f.reshape(n, d)


def _expert_kernel(x_ref, wg_ref, wu_ref, wd_ref, gate_ref, o_ref):
    f = pl.program_id(2)

    @pl.when(f == 0)
    def _():
        o_ref[...] = jnp.zeros_like(o_ref)

    x = x_ref[0]
    a = jnp.dot(x, wg_ref[0].astype(BF16), preferred_element_type=F32)
    u = jnp.dot(x, wu_ref[0].astype(BF16), preferred_element_type=F32)
    hmid = (a / (1.0 + jnp.exp(-a)) * u).astype(BF16)
    o_ref[0] += jnp.dot(hmid, wd_ref[0].astype(BF16), preferred_element_type=F32)

    @pl.when(f == pl.num_programs(2) - 1)
    def _():
        o_ref[0] = o_ref[0] * gate_ref[0]


def _expert_ffn(xs, w_gate, w_up, w_down, gates):
    e, cap, d = xs.shape
    ff = w_gate.shape[2]
    tm = _pick(cap, (1024, 512, 256, 128, 64, 8))
    tf = _pick(ff, (256, 128))
    return pl.pallas_call(
        _expert_kernel,
        grid=(e, cap // tm, ff // tf),
        in_specs=[pl.BlockSpec((1, tm, d), lambda ei, i, f: (ei, i, 0)),
                  pl.BlockSpec((1, d, tf), lambda ei, i, f: (ei, 0, f)),
                  pl.BlockSpec((1, d, tf), lambda ei, i, f: (ei, 0, f)),
                  pl.BlockSpec((1, tf, d), lambda ei, i, f: (ei, f, 0)),
                  pl.BlockSpec((1, tm, 1), lambda ei, i, f: (ei, i, 0))],
        out_specs=pl.BlockSpec((1, tm, d), lambda ei, i, f: (ei, i, 0)),
        out_shape=jax.ShapeDtypeStruct((e, cap, d), F32),
        compiler_params=_cparams(("parallel", "parallel", "arbitrary")),
        name="expert_ffn",
    )(xs, w_gate, w_up, w_down, gates)


def _ec_moe(hf, aff, w_gate, w_up, w_down):
    n, d = hf.shape
    e = w_gate.shape[0]
    cap = EC_CAPACITY * n // e
    gates, idx = lax.top_k(aff[:, :e].T, cap)
    xs = jnp.take(hf, idx.reshape(-1), axis=0).reshape(e, cap, d)
    y = _expert_ffn(xs, w_gate, w_up, w_down, gates[..., None])
    return jnp.zeros((n, d), F32).at[idx.reshape(-1)].add(y.reshape(-1, d))


def _rope_table(n_ctx, n):
    t = jnp.arange(n)
    n_freq = MLA_ROPE // 4
    inv_freq = ROPE_THETA ** (-jnp.arange(n_freq, dtype=F32) / n_freq)
    ar = (t // GRID_W).astype(F32)[:, None] * inv_freq
    ac = (t % GRID_W).astype(F32)[:, None] * inv_freq
    cos = jnp.concatenate([jnp.cos(ar), jnp.cos(ar), jnp.cos(ac), jnp.cos(ac)], axis=1)
    sin = jnp.concatenate([-jnp.sin(ar), jnp.sin(ar), -jnp.sin(ac), jnp.sin(ac)], axis=1)
    tab = jnp.concatenate([cos, sin], axis=1)
    ident = jnp.concatenate([jnp.ones((n_ctx, MLA_ROPE), F32), jnp.zeros((n_ctx, MLA_ROPE), F32)], axis=1)
    return jnp.concatenate([ident, tab], axis=0)


def _swap_pairs(w):
    q = MLA_ROPE // 4
    return jnp.concatenate([w[:, q:2 * q], w[:, :q], w[:, 3 * q:], w[:, 2 * q:3 * q]], axis=1)


def kernel(x, c, ctx, c_ctx, ada_w, ada_b, g_mix, g_ffn, g_final, w_in, g_qn, w_uq, g_kvn, w_ukv,
           na_rel_bias, w_out_attn, w_out_fourier, w_router, w_gate, w_up, w_down):
    _, n, d = x.shape
    n_ctx = ctx.shape[1]
    heads = d // 256
    q_rank = d // 4
    kv_rank = d // 8
    na_width = heads * NA_DIM
    mla_scale = (MLA_NOPE + MLA_ROPE) ** -0.5
    na_scale = NA_DIM ** -0.5
    xs = x[0]
    cx = ctx[0]

    cvecs = jnp.zeros((8, d), F32).at[0].set(c[0]).at[1].set(c_ctx)

    m0 = _ada_terms(cvecs, ada_w[0], ada_b[0])
    sh_m, sc_m, g_m, sh_f, sc_f, g_f = [m0[0:1, k * d:(k + 1) * d] for k in range(6)]
    csh_m, csc_m = m0[1:2, 0:d], m0[1:2, d:2 * d]
    gm0 = g_mix[0][None]
    h_x = _modulate(xs, gm0, sh_m, sc_m)
    h_c = _modulate(cx, gm0, csh_m, csc_m)
    h_all = jnp.concatenate([h_c, h_x], axis=0)

    w0 = w_in[0]
    o1, o2, o3 = q_rank, q_rank + kv_rank, q_rank + kv_rank + MLA_ROPE
    w_kr = w0[:, o2:o3]
    w_a = jnp.concatenate([w0[:, :o2], w_kr, _swap_pairs(w_kr)], axis=1).astype(BF16)
    w_q = w0[:, o3:o3 + na_width].astype(BF16)
    w_kv = w0[:, o3 + na_width:].astype(BF16)
    ones = lambda k: jnp.ones((1, k), F32)
    a_all = _matmul(h_all, w_a, ones(w_a.shape[1]), F32)
    q_na = _matmul(h_x, w_q, jnp.full((1, na_width), na_scale, F32), BF16)
    kv_na = _matmul(h_all, w_kv, ones(2 * na_width), BF16)

    cs = _rope_table(n_ctx, n)
    wq3 = w_uq[0].reshape(q_rank, heads, MLA_NOPE + MLA_ROPE)
    wq_rope = wq3[:, :, MLA_NOPE:]
    wq_swap = jnp.stack([_swap_pairs(wq_rope[:, h]) for h in range(heads)], axis=1)
    wq = jnp.concatenate([wq3, wq_swap], axis=2).transpose(1, 0, 2).astype(BF16)
    wkv = w_ukv[0].reshape(kv_rank, heads, MLA_NOPE + MLA_V).transpose(1, 0, 2).astype(BF16)
    tmp = _pick(math.gcd(n, n_ctx), (256, 128, 64, 8))
    q_mla = _qprep(a_all, n_ctx // tmp, n, g_qn[0][None], wq, cs, heads, q_rank, mla_scale, tmp)
    k_mla, v_mla = _kvprep(a_all, g_kvn[0][None], wkv, cs, heads, q_rank, kv_rank, tmp)
    o_a = _mla_attention(q_mla, k_mla, v_mla)

    table = _na_bias_table(na_rel_bias[0])
    o_b = _na_attention(q_na, kv_na, table, heads, n_ctx)
    o_cat = jnp.concatenate([o_a, o_b], axis=1)
    x1 = _matmul_residual(o_cat, w_out_attn[0].astype(BF16), xs, g_m)

    hf, aff = _modulate_router(x1, g_ffn[0][None], sh_f, sc_f, w_router[0])
    moe0 = _ec_moe(hf, aff, w_gate[0], w_up[0], w_down[0])

    m1 = _ada_terms(cvecs, ada_w[1], ada_b[1])
    sh_m1, sc_m1, g_m1, sh_f1, sc_f1, g_f1 = [m1[0:1, k * d:(k + 1) * d] for k in range(6)]
    x2, fr = _fourier_transform(x1, g_f, moe0, g_mix[1][None], sh_m1, sc_m1)
    x3 = _matmul_residual(fr, w_out_fourier[0].astype(BF16), x2, g_m1)
    hf1, aff1 = _modulate_router(x3, g_ffn[1][None], sh_f1, sc_f1, w_router[1])
    moe1 = _ec_moe(hf1, aff1, w_gate[1], w_up[1], w_down[1])
    out = _final_norm(x3, g_f1, moe1, g_final[None])
    return out[None]
```

```python
import functools
import math

import jax
import jax.numpy as jnp
from jax import lax
from jax.experimental import pallas as pl
from jax.experimental.pallas import tpu as pltpu

F32 = jnp.float32
BF16 = jnp.bfloat16

GRID_W = 64
NORM_EPS = 1e-6
MLA_NOPE = 128
MLA_ROPE = 64
MLA_V = 128
MLA_QK_PAD = 256
ROPE_THETA = 10000.0
NA_DIM = 128
NA_WIN_H = 8
NA_WIN_W = 16
NA_ROW_BLOCK = 8
NA_KEY_ROWS = 16
FNET_GROUPS = 8
N_EXPERTS = 16
EC_CAPACITY = 2
ROUTER_LANES = 128
NEG_BIG = -1e30
LOG2_E = math.log2(math.e)

V7X_VMEM_LIMIT_BYTES = 56 * 1024 * 1024


def _cparams(sem):
    return pltpu.CompilerParams(dimension_semantics=sem, vmem_limit_bytes=V7X_VMEM_LIMIT_BYTES)


def _pick(n, candidates):
    for c in candidates:
        if n % c == 0:
            return c
    return n


def _ada_kernel(c_ref, w_ref, b_ref, o_ref):
    c = c_ref[...]
    s = c / (1.0 + jnp.exp(-c))
    o_ref[...] = jnp.dot(s, w_ref[...], preferred_element_type=F32,
                         precision=lax.Precision.HIGHEST) + b_ref[...]


def _ada_terms(cvecs, w, b, layer):
    depth, d, n6 = w.shape
    tn = _pick(n6, (1024, 512, 256, 128))
    return pl.pallas_call(
        _ada_kernel,
        grid=(n6 // tn,),
        in_specs=[pl.BlockSpec((8, d), lambda j: (0, 0)),
                  pl.BlockSpec((None, d, tn), lambda j: (layer, 0, j)),
                  pl.BlockSpec((None, 1, tn), lambda j: (layer, 0, j))],
        out_specs=pl.BlockSpec((8, tn), lambda j: (0, j)),
        out_shape=jax.ShapeDtypeStruct((8, n6), F32),
        compiler_params=_cparams(("parallel",)),
        name="ada_terms",
    )(cvecs, w, b.reshape(depth, 1, n6))


def _modulated(x, g, shift, scale):
    ms = jnp.mean(x * x, axis=-1, keepdims=True)
    y = x * lax.rsqrt(ms + NORM_EPS) * g
    return y * (1.0 + scale) + shift


def _mod_kernel(x_ref, g_ref, sh_ref, sc_ref, h_ref):
    h_ref[...] = _modulated(x_ref[...], g_ref[...], sh_ref[...], sc_ref[...]).astype(h_ref.dtype)


def _modulate(x, g, shift, scale, out_dtype=BF16):
    n, d = x.shape
    tm = _pick(n, (512, 256, 128, 64, 8))
    row = pl.BlockSpec((tm, d), lambda i: (i, 0))
    vec = pl.BlockSpec((1, d), lambda i: (0, 0))
    return pl.pallas_call(
        _mod_kernel,
        grid=(n // tm,),
        in_specs=[row, vec, vec, vec],
        out_specs=row,
        out_shape=jax.ShapeDtypeStruct((n, d), out_dtype),
        compiler_params=_cparams(("parallel",)),
        name="modulate",
    )(x, g, shift, scale)


def _router_kernel(n_experts, x_ref, g_ref, sh_ref, sc_ref, wr_ref, h_ref, aff_ref):
    h = _modulated(x_ref[...], g_ref[...], sh_ref[...], sc_ref[...])
    h_ref[...] = h.astype(h_ref.dtype)
    logits = jnp.dot(h, wr_ref[...], preferred_element_type=F32, precision=lax.Precision.HIGHEST)
    lane = lax.broadcasted_iota(jnp.int32, logits.shape, 1)
    logits = jnp.where(lane < n_experts, logits, NEG_BIG)
    m = jnp.max(logits, axis=-1, keepdims=True)
    p = jnp.exp(logits - m)
    aff_ref[...] = p / jnp.sum(p, axis=-1, keepdims=True)


def _modulate_router(x, g, shift, scale, w_router):
    n, d = x.shape
    n_experts = w_router.shape[1]
    wr = jnp.zeros((d, ROUTER_LANES), F32).at[:, :n_experts].set(w_router)
    tm = _pick(n, (512, 256, 128, 64, 8))
    row = pl.BlockSpec((tm, d), lambda i: (i, 0))
    vec = pl.BlockSpec((1, d), lambda i: (0, 0))
    return pl.pallas_call(
        functools.partial(_router_kernel, n_experts),
        grid=(n // tm,),
        in_specs=[row, vec, vec, vec, pl.BlockSpec((d, ROUTER_LANES), lambda i: (0, 0))],
        out_specs=[row, pl.BlockSpec((tm, ROUTER_LANES), lambda i: (i, 0))],
        out_shape=[jax.ShapeDtypeStruct((n, d), BF16),
                   jax.ShapeDtypeStruct((n, ROUTER_LANES), F32)],
        compiler_params=_cparams(("parallel",)),
        name="modulate_router",
    )(x, g, shift, scale, wr)


def _resid_kernel(x_ref, gate_ref, y_ref, o_ref):
    o_ref[...] = x_ref[...] + gate_ref[...] * y_ref[...]


def _residual(x, gate, y):
    n, d = x.shape
    tm = _pick(n, (512, 256, 128, 64, 8))
    row = pl.BlockSpec((tm, d), lambda i: (i, 0))
    vec = pl.BlockSpec((1, d), lambda i: (0, 0))
    return pl.pallas_call(
        _resid_kernel,
        grid=(n // tm,),
        in_specs=[row, vec, row],
        out_specs=row,
        out_shape=jax.ShapeDtypeStruct((n, d), F32),
        compiler_params=_cparams(("parallel",)),
        name="residual",
    )(x, gate, y)


def _final_kernel(x_ref, gate_ref, y_ref, g_ref, o_ref):
    x = x_ref[...] + gate_ref[...] * y_ref[...]
    ms = jnp.mean(x * x, axis=-1, keepdims=True)
    o_ref[...] = x * lax.rsqrt(ms + NORM_EPS) * g_ref[...]


def _final_norm(x, gate, y, g):
    n, d = x.shape
    tm = _pick(n, (512, 256, 128, 64, 8))
    row = pl.BlockSpec((tm, d), lambda i: (i, 0))
    vec = pl.BlockSpec((1, d), lambda i: (0, 0))
    return pl.pallas_call(
        _final_kernel,
        grid=(n // tm,),
        in_specs=[row, vec, row, vec],
        out_specs=row,
        out_shape=jax.ShapeDtypeStruct((n, d), F32),
        compiler_params=_cparams(("parallel",)),
        name="final_norm",
    )(x, gate, y, g)


def _mm_kernel(a_ref, w_ref, cs_ref, o_ref):
    acc = jnp.dot(a_ref[...], w_ref[...], preferred_element_type=F32)
    o_ref[...] = (acc * cs_ref[...]).astype(o_ref.dtype)


def _mm_resid_kernel(a1_ref, a2_ref, w_ref, x_ref, gate_ref, o_ref):
    k1 = a1_ref.shape[1]
    acc = jnp.dot(a1_ref[...], w_ref[:k1, :], preferred_element_type=F32)
    acc = acc + jnp.dot(a2_ref[...], w_ref[k1:, :], preferred_element_type=F32)
    o_ref[...] = x_ref[...] + gate_ref[...] * acc


def _mm_tiles(m, n):
    tm = _pick(m, (1280, 1024, 512, 256, 128, 64, 8))
    tn = n if n <= 1024 else _pick(n, (1024, 512, 256, 128))
    return tm, tn


def _matmul(a, w, col_scale, out_dtype):
    m, k = a.shape
    n = w.shape[1]
    tm, tn = _mm_tiles(m, n)
    return pl.pallas_call(
        _mm_kernel,
        grid=(m // tm, n // tn),
        in_specs=[pl.BlockSpec((tm, k), lambda i, j: (i, 0)),
                  pl.BlockSpec((k, tn), lambda i, j: (0, j)),
                  pl.BlockSpec((1, tn), lambda i, j: (0, j))],
        out_specs=pl.BlockSpec((tm, tn), lambda i, j: (i, j)),
        out_shape=jax.ShapeDtypeStruct((m, n), out_dtype),
        compiler_params=_cparams(("parallel", "parallel")),
        name="matmul",
    )(a, w, col_scale)


def _matmul_residual(a1, a2, w, x, gate):
    m = a1.shape[0]
    n = w.shape[1]
    k1 = w.shape[0] // 2
    second_half = 0
    if a2 is None:
        a2, second_half = a1, 1
    tm, tn = _mm_tiles(m, n)
    return pl.pallas_call(
        _mm_resid_kernel,
        grid=(m // tm, n // tn),
        in_specs=[pl.BlockSpec((tm, k1), lambda i, j: (i, 0)),
                  pl.BlockSpec((tm, k1), lambda i, j: (i, second_half)),
                  pl.BlockSpec((2 * k1, tn), lambda i, j: (0, j)),
                  pl.BlockSpec((tm, tn), lambda i, j: (i, j)),
                  pl.BlockSpec((1, tn), lambda i, j: (0, j))],
        out_specs=pl.BlockSpec((tm, tn), lambda i, j: (i, j)),
        out_shape=jax.ShapeDtypeStruct((m, n), F32),
        compiler_params=_cparams(("parallel", "parallel")),
        name="matmul_residual",
    )(a1, a2, w, x, gate)


def _rms(x, g):
    ms = jnp.mean(x * x, axis=-1, keepdims=True)
    return x * lax.rsqrt(ms + NORM_EPS) * g


def _rope_half(pair, cs):
    t = pair * cs
    t = t + pltpu.roll(t, MLA_ROPE, 1)
    lane = lax.broadcasted_iota(jnp.int32, t.shape, 1)
    return jnp.where(lane < MLA_ROPE, t, 0.0)


def _qprep_kernel(heads, q_rank, scale, a_ref, g_ref, w_ref, cs_ref, q_ref):
    cq = _rms(a_ref[:, :q_rank], g_ref[...]).astype(BF16)
    cs = cs_ref[...]
    for h in range(heads):
        qh = jnp.dot(cq, w_ref[h], preferred_element_type=F32)
        q = jnp.concatenate([qh[:, :MLA_NOPE], _rope_half(qh[:, MLA_NOPE:], cs)], axis=1) * scale
        q_ref[h] = q.T.astype(BF16)


def _kvprep_kernel(heads, q_rank, kv_rank, a_ref, g_ref, w_ref, cs_ref, k_ref, v_ref):
    ckv = _rms(a_ref[:, q_rank:q_rank + kv_rank], g_ref[...]).astype(BF16)
    kr = _rope_half(a_ref[:, q_rank + kv_rank:], cs_ref[...]).astype(BF16)
    for h in range(heads):
        kvh = jnp.dot(ckv, w_ref[h], preferred_element_type=F32)
        k_ref[h, :, :MLA_NOPE] = kvh[:, :MLA_NOPE].astype(BF16)
        k_ref[h, :, MLA_NOPE:] = kr
        v_ref[h] = kvh[:, MLA_NOPE:].T.astype(BF16)


def _qprep(a, row_off_blocks, n_rows, g_qn, wq, cs, heads, q_rank, scale, tm):
    aw = a.shape[1]
    return pl.pallas_call(
        functools.partial(_qprep_kernel, heads, q_rank, scale),
        grid=(n_rows // tm,),
        in_specs=[pl.BlockSpec((tm, aw), lambda i: (i + row_off_blocks, 0)),
                  pl.BlockSpec((1, q_rank), lambda i: (0, 0)),
                  pl.BlockSpec((heads, q_rank, MLA_QK_PAD), lambda i: (0, 0, 0)),
                  pl.BlockSpec((tm, 2 * MLA_ROPE), lambda i: (i + row_off_blocks, 0))],
        out_specs=pl.BlockSpec((heads, MLA_QK_PAD, tm), lambda i: (0, 0, i)),
        out_shape=jax.ShapeDtypeStruct((heads, MLA_QK_PAD, n_rows), BF16),
        compiler_params=_cparams(("parallel",)),
        name="mla_q_prep",
    )(a, g_qn, wq, cs)


def _kvprep(a, g_kvn, wkv, cs, heads, q_rank, kv_rank, tm):
    n_rows, aw = a.shape
    return pl.pallas_call(
        functools.partial(_kvprep_kernel, heads, q_rank, kv_rank),
        grid=(n_rows // tm,),
        in_specs=[pl.BlockSpec((tm, aw), lambda i: (i, 0)),
                  pl.BlockSpec((1, kv_rank), lambda i: (0, 0)),
                  pl.BlockSpec((heads, kv_rank, MLA_NOPE + MLA_V), lambda i: (0, 0, 0)),
                  pl.BlockSpec((tm, 2 * MLA_ROPE), lambda i: (i, 0))],
        out_specs=[pl.BlockSpec((heads, tm, MLA_QK_PAD), lambda i: (0, i, 0)),
                   pl.BlockSpec((heads, MLA_V, tm), lambda i: (0, 0, i))],
        out_shape=[jax.ShapeDtypeStruct((heads, n_rows, MLA_QK_PAD), BF16),
                   jax.ShapeDtypeStruct((heads, MLA_V, n_rows), BF16)],
        compiler_params=_cparams(("parallel",)),
        name="mla_kv_prep",
    )(a, g_kvn, wkv, cs)


def _dot_nt(a, b):
    return lax.dot_general(a, b, (((1,), (1,)), ((), ())), preferred_element_type=F32)


def _mla_kernel(tk, n_kv, n_sub, qt_ref, k_ref, vt_ref, o_ref):
    tq = qt_ref.shape[2] // n_sub

    def body(j, carry):
        start = pl.multiple_of(j * tk, tk)
        kc = k_ref[0, pl.ds(start, tk), :]
        vc = vt_ref[0, :, pl.ds(start, tk)]
        out = []
        scores = [jnp.dot(kc, qt_ref[0, :, u * tq:(u + 1) * tq], preferred_element_type=F32)
                  for u in range(n_sub)]
        for u in range(n_sub):
            m, l, acc = carry[u]
            s = scores[u]
            m_new = jnp.maximum(m, jnp.max(s, axis=0, keepdims=True))
            p = jnp.exp2(s - m_new)
            alpha = jnp.exp2(m - m_new)
            l = alpha * l + jnp.sum(p, axis=0, keepdims=True)
            acc = alpha * acc + jnp.dot(vc, p.astype(BF16), preferred_element_type=F32)
            out.append((m_new, l, acc))
        return tuple(out)

    init = tuple((jnp.full((1, tq), NEG_BIG, F32), jnp.zeros((1, tq), F32), jnp.zeros((MLA_V, tq), F32))
                 for _ in range(n_sub))
    res = lax.fori_loop(0, n_kv // tk, body, init)
    for u in range(n_sub):
        _, l, acc = res[u]
        o_ref[u * tq:(u + 1) * tq, :] = (acc / l).T.astype(o_ref.dtype)


def _mla_attention(qt, k, vt):
    heads, _, n = qt.shape
    n_kv = k.shape[1]
    tq = _pick(n, (256, 128))
    n_sub = 4 if n % (4 * tq) == 0 else 1
    tk = _pick(n_kv, (1280, 768, 512, 256, 128))
    return pl.pallas_call(
        functools.partial(_mla_kernel, tk, n_kv, n_sub),
        grid=(heads, n // (tq * n_sub)),
        in_specs=[pl.BlockSpec((1, MLA_QK_PAD, tq * n_sub), lambda h, i: (h, 0, i)),
                  pl.BlockSpec((1, n_kv, MLA_QK_PAD), lambda h, i: (h, 0, 0)),
                  pl.BlockSpec((1, MLA_V, n_kv), lambda h, i: (h, 0, 0))],
        out_specs=pl.BlockSpec((tq * n_sub, MLA_V), lambda h, i: (i, h)),
        out_shape=jax.ShapeDtypeStruct((n, heads * MLA_V), BF16),
        compiler_params=_cparams(("parallel", "parallel")),
        name="mla_attention",
    )(qt, k, vt)


def _na_kernel(n_ctx, rows_total, win_h, key_rows, q_ref, k_ref, v_ref, t_ref, o_ref, bm_ref):
    b = pl.program_id(1)
    nb = pl.num_programs(1)
    r0 = b * NA_ROW_BLOCK
    base = jnp.clip(r0 - NA_WIN_H // 2, 0, rows_total - key_rows)

    @pl.when((b <= 1) | (b == nb - 1))
    def _():
        for i in range(NA_ROW_BLOCK):
            rs = jnp.clip(r0 + i - win_h // 2, 0, rows_total - win_h)
            for j in range(key_rows):
                kr = base + j
                valid = (kr >= rs) & (kr < rs + win_h)
                idx = jnp.clip(kr - (r0 + i) + NA_WIN_H - 1, 0, 2 * NA_WIN_H - 2)
                bm_ref[i * GRID_W:(i + 1) * GRID_W, j * GRID_W:(j + 1) * GRID_W] = jnp.where(
                    valid, t_ref[0, idx], NEG_BIG)

    q = q_ref[...]
    n_loc = key_rows * GRID_W
    loc0 = pl.multiple_of(n_ctx + base * GRID_W, GRID_W)
    s_ctx = _dot_nt(q, k_ref[0:n_ctx, :])
    s_loc = _dot_nt(q, k_ref[pl.ds(loc0, n_loc), :]) + bm_ref[...]
    m = jnp.maximum(jnp.max(s_ctx, axis=-1, keepdims=True), jnp.max(s_loc, axis=-1, keepdims=True))
    p_ctx = jnp.exp(s_ctx - m)
    p_loc = jnp.exp(s_loc - m)
    l = jnp.sum(p_ctx, axis=-1, keepdims=True) + jnp.sum(p_loc, axis=-1, keepdims=True)
    o = jnp.dot(p_ctx.astype(BF16), v_ref[0:n_ctx, :], preferred_element_type=F32)
    o = o + jnp.dot(p_loc.astype(BF16), v_ref[pl.ds(loc0, n_loc), :], preferred_element_type=F32)
    o_ref[...] = (o / l).astype(o_ref.dtype)


def _na_bias_table(rel_bias):
    cq = jnp.arange(GRID_W)
    ck = jnp.arange(GRID_W)
    col_start = jnp.clip(cq - NA_WIN_W // 2, 0, GRID_W - NA_WIN_W)
    valid = (ck[None, :] >= col_start[:, None]) & (ck[None, :] < col_start[:, None] + NA_WIN_W)
    co = jnp.clip(ck[None, :] - cq[:, None] + NA_WIN_W - 1, 0, 2 * NA_WIN_W - 2)
    onehot = (co[None] == jnp.arange(2 * NA_WIN_W - 1)[:, None, None]).astype(F32)
    t = jnp.einsum('hrc,cqk->hrqk', rel_bias, onehot, precision=lax.Precision.HIGHEST)
    return jnp.where(valid[None, None], t, NEG_BIG)


def _na_attention(q, kv, table, heads, n_ctx):
    n = q.shape[0]
    n_kv = kv.shape[0]
    rows_total = n // GRID_W
    win_h = min(NA_WIN_H, rows_total)
    key_rows = min(NA_KEY_ROWS, rows_total)
    tq = NA_ROW_BLOCK * GRID_W
    n_off = table.shape[1]
    return pl.pallas_call(
        functools.partial(_na_kernel, n_ctx, rows_total, win_h, key_rows),
        grid=(heads, n // tq),
        in_specs=[pl.BlockSpec((tq, NA_DIM), lambda h, i: (i, h)),
                  pl.BlockSpec((n_kv, NA_DIM), lambda h, i: (0, h)),
                  pl.BlockSpec((n_kv, NA_DIM), lambda h, i: (0, heads + h)),
                  pl.BlockSpec((1, n_off, GRID_W, GRID_W), lambda h, i: (h, 0, 0, 0))],
        out_specs=pl.BlockSpec((tq, NA_DIM), lambda h, i: (i, h)),
        out_shape=jax.ShapeDtypeStruct((n, heads * NA_DIM), BF16),
        scratch_shapes=[pltpu.VMEM((tq, key_rows * GRID_W), F32)],
        compiler_params=_cparams(("parallel", "arbitrary")),
        name="na_attention",
    )(q, kv, kv, table)


def _fft_a_kernel(groups, x_ref, gate_ref, y_ref, g_ref, sh_ref, sc_ref, w_ref, xo_ref, zr_ref, zi_ref):
    x = x_ref[...] + gate_ref[...] * y_ref[...]
    xo_ref[...] = x
    h = _modulated(x, g_ref[...], sh_ref[...], sc_ref[...]).astype(BF16)
    dg = h.shape[1] // groups
    for g in range(groups):
        z = jnp.dot(h[:, g * dg:(g + 1) * dg], w_ref[...], preferred_element_type=F32)
        zr_ref[:, g * dg:(g + 1) * dg] = z[:, :dg].astype(BF16)
        zi_ref[:, g * dg:(g + 1) * dg] = z[:, dg:].astype(BF16)


def _fft_b_kernel(zr_ref, zi_ref, w_ref, yr_ref, yi_ref):
    n1 = zr_ref.shape[0]
    z = jnp.concatenate([zr_ref[...], zi_ref[...]], axis=0)
    y = jnp.dot(w_ref[...], z, preferred_element_type=F32)
    yr_ref[...] = y[:n1].astype(BF16)
    yi_ref[...] = y[n1:].astype(BF16)


def _fft_c_kernel(yr_ref, yi_ref, w_ref, o_ref):
    y = jnp.concatenate([yr_ref[0], yi_ref[0]], axis=0)
    o_ref[...] = jnp.dot(w_ref[0], y, preferred_element_type=F32).astype(o_ref.dtype)


def _dft_tables(n, dg):
    lg = int(math.log2(n))
    n1 = 1 << ((lg + 1) // 2)
    n2 = n // n1
    scale_a = 1.0 / math.sqrt(dg)
    sb = 1 << (int(math.log2(n1)) // 2)
    scale_b = 1.0 / sb
    scale_c = 1.0 / (math.sqrt(n) / sb)
    ch = jnp.arange(dg, dtype=jnp.int32)
    ang = (2.0 * math.pi / dg) * ((ch[:, None] * ch[None, :]) % dg).astype(F32)
    wa = jnp.concatenate([jnp.cos(ang), -jnp.sin(ang)], axis=1) * scale_a
    a = jnp.arange(n1, dtype=jnp.int32)
    ang = (2.0 * math.pi / n1) * ((a[:, None] * a[None, :]) % n1).astype(F32)
    c, s = jnp.cos(ang), jnp.sin(ang)
    wb = jnp.concatenate([jnp.concatenate([c, s], axis=1), jnp.concatenate([-s, c], axis=1)], axis=0) * scale_b
    k1 = jnp.arange(n1, dtype=jnp.int32)[:, None, None]
    k2 = jnp.arange(n2, dtype=jnp.int32)[None, :, None]
    b = jnp.arange(n2, dtype=jnp.int32)[None, None, :]
    ang = (2.0 * math.pi / n) * ((b * (k1 + n1 * k2)) % n).astype(F32)
    wc = jnp.concatenate([jnp.cos(ang), jnp.sin(ang)], axis=2) * scale_c
    return n1, n2, wa.astype(BF16), wb.astype(BF16), wc.astype(BF16)


def _fourier_transform(x, gate, y, g, shift, scale):
    n, d = x.shape
    dg = d // FNET_GROUPS
    n1, n2, wa, wb, wc = _dft_tables(n, dg)
    tm = _pick(n, (512, 256, 128, 64, 8))
    row = pl.BlockSpec((tm, d), lambda i: (i, 0))
    vec = pl.BlockSpec((1, d), lambda i: (0, 0))
    x_new, zr, zi = pl.pallas_call(
        functools.partial(_fft_a_kernel, FNET_GROUPS),
        grid=(n // tm,),
        in_specs=[row, vec, row, vec, vec, vec, pl.BlockSpec((dg, 2 * dg), lambda i: (0, 0))],
        out_specs=[row, row, row],
        out_shape=[jax.ShapeDtypeStruct((n, d), F32),
                   jax.ShapeDtypeStruct((n, d), BF16),
                   jax.ShapeDtypeStruct((n, d), BF16)],
        compiler_params=_cparams(("parallel",)),
        name="fft_channels",
    )(x, gate, y, g, shift, scale, wa)

    cols = n2 * d
    tn = _pick(cols, (4096, 2048, 1024, 512, 256, 128))
    col = pl.BlockSpec((n1, tn), lambda j: (0, j))
    yr, yi = pl.pallas_call(
        _fft_b_kernel,
        grid=(cols // tn,),
        in_specs=[col, col, pl.BlockSpec((2 * n1, 2 * n1), lambda j: (0, 0))],
        out_specs=[col, col],
        out_shape=[jax.ShapeDtypeStruct((n1, cols), BF16)] * 2,
        compiler_params=_cparams(("parallel",)),
        name="fft_pos_outer",
    )(zr.reshape(n1, cols), zi.reshape(n1, cols), wb)

    slab = pl.BlockSpec((1, n2, d), lambda k: (k, 0, 0))
    f = pl.pallas_call(
        _fft_c_kernel,
        grid=(n1,),
        in_specs=[slab, slab, pl.BlockSpec((1, n2, 2 * n2), lambda k: (k, 0, 0))],
        out_specs=pl.BlockSpec((n2, d), lambda k: (0, k)),
        out_shape=jax.ShapeDtypeStruct((n2, n1 * d), BF16),
        compiler_params=_cparams(("parallel",)),
        name="fft_pos_inner",
    )(yr.reshape(n1, n2, d), yi.reshape(n1, n2, d), wc)
    return x_new, f.reshape(n, d)


def _expert_kernel(x_ref, wg_ref, wu_ref, wd_ref, gate_ref, o_ref):
    f = pl.program_id(2)

    @pl.when(f == 0)
    def _():
        o_ref[...] = jnp.zeros_like(o_ref)

    x = x_ref[0]
    a = jnp.dot(x, wg_ref[...].astype(BF16), preferred_element_type=F32)
    u = jnp.dot(x, wu_ref[...].astype(BF16), preferred_element_type=F32)
    hmid = (a / (1.0 + jnp.exp(-a)) * u).astype(BF16)
    o_ref[0] += jnp.dot(hmid, wd_ref[...].astype(BF16), preferred_element_type=F32)

    @pl.when(f == pl.num_programs(2) - 1)
    def _():
        o_ref[0] = o_ref[0] * gate_ref[0]


def _expert_ffn(xs, w_gate, w_up, w_down, gates, layer):
    e, cap, d = xs.shape
    ff = w_gate.shape[3]
    tm = _pick(cap, (1024, 512, 256, 128, 64, 8))
    tf = _pick(ff, (256, 128))
    return pl.pallas_call(
        _expert_kernel,
        grid=(e, cap // tm, ff // tf),
        in_specs=[pl.BlockSpec((1, tm, d), lambda ei, i, f: (ei, i, 0)),
                  pl.BlockSpec((None, None, d, tf), lambda ei, i, f: (layer, ei, 0, f)),
                  pl.BlockSpec((None, None, d, tf), lambda ei, i, f: (layer, ei, 0, f)),
                  pl.BlockSpec((None, None, tf, d), lambda ei, i, f: (layer, ei, f, 0)),
                  pl.BlockSpec((1, tm, 1), lambda ei, i, f: (ei, i, 0))],
        out_specs=pl.BlockSpec((1, tm, d), lambda ei, i, f: (ei, i, 0)),
        out_shape=jax.ShapeDtypeStruct((e, cap, d), F32),
        compiler_params=_cparams(("parallel", "parallel", "arbitrary")),
        name="expert_ffn",
    )(xs, w_gate, w_up, w_down, gates)


def _ec_moe(hf, aff, w_gate, w_up, w_down, layer):
    n, d = hf.shape
    e = w_gate.shape[1]
    cap = EC_CAPACITY * n // e
    gates, idx = lax.top_k(aff[:, :e].T, cap)
    xs = jnp.take(hf, idx.reshape(-1), axis=0).reshape(e, cap, d)
    y = _expert_ffn(xs, w_gate, w_up, w_down, gates[..., None], layer)
    return jnp.zeros((n, d), F32).at[idx.reshape(-1)].add(y.reshape(-1, d))


def _rope_table(n_ctx, n):
    t = jnp.arange(n)
    n_freq = MLA_ROPE // 4
    inv_freq = ROPE_THETA ** (-jnp.arange(n_freq, dtype=F32) / n_freq)
    ar = (t // GRID_W).astype(F32)[:, None] * inv_freq
    ac = (t % GRID_W).astype(F32)[:, None] * inv_freq
    cos = jnp.concatenate([jnp.cos(ar), jnp.cos(ar), jnp.cos(ac), jnp.cos(ac)], axis=1)
    sin = jnp.concatenate([-jnp.sin(ar), jnp.sin(ar), -jnp.sin(ac), jnp.sin(ac)], axis=1)
    tab = jnp.concatenate([cos, sin], axis=1)
    ident = jnp.concatenate([jnp.ones((n_ctx, MLA_ROPE), F32), jnp.zeros((n_ctx, MLA_ROPE), F32)], axis=1)
    return jnp.concatenate([ident, tab], axis=0)


def _swap_pairs(w):
    q = MLA_ROPE // 4
    return jnp.concatenate([w[:, q:2 * q], w[:, :q], w[:, 3 * q:], w[:, 2 * q:3 * q]], axis=1)


def kernel(x, c, ctx, c_ctx, ada_w, ada_b, g_mix, g_ffn, g_final, w_in, g_qn, w_uq, g_kvn, w_ukv,
           na_rel_bias, w_out_attn, w_out_fourier, w_router, w_gate, w_up, w_down):
    _, n, d = x.shape
    n_ctx = ctx.shape[1]
    heads = d // 256
    q_rank = d // 4
    kv_rank = d // 8
    na_width = heads * NA_DIM
    mla_scale = (MLA_NOPE + MLA_ROPE) ** -0.5
    na_scale = NA_DIM ** -0.5
    xs = x[0]
    cx = ctx[0]

    cvecs = jnp.zeros((8, d), F32).at[0].set(c[0]).at[1].set(c_ctx)

    m0 = _ada_terms(cvecs, ada_w, ada_b, 0)
    sh_m, sc_m, g_m, sh_f, sc_f, g_f = [m0[0:1, k * d:(k + 1) * d] for k in range(6)]
    csh_m, csc_m = m0[1:2, 0:d], m0[1:2, d:2 * d]
    gm0 = g_mix[0][None]
    h_x = _modulate(xs, gm0, sh_m, sc_m)
    h_c = _modulate(cx, gm0, csh_m, csc_m)
    h_all = jnp.concatenate([h_c, h_x], axis=0)

    w0 = w_in[0]
    o1, o2, o3 = q_rank, q_rank + kv_rank, q_rank + kv_rank + MLA_ROPE
    w_kr = w0[:, o2:o3]
    w_a = jnp.concatenate([w0[:, :o2], w_kr, _swap_pairs(w_kr)], axis=1).astype(BF16)
    w_q = w0[:, o3:o3 + na_width].astype(BF16)
    w_kv = w0[:, o3 + na_width:].astype(BF16)
    ones = lambda k: jnp.ones((1, k), F32)
    a_all = _matmul(h_all, w_a, ones(w_a.shape[1]), F32)
    q_na = _matmul(h_x, w_q, jnp.full((1, na_width), na_scale, F32), BF16)
    kv_na = _matmul(h_all, w_kv, ones(2 * na_width), BF16)

    cs = _rope_table(n_ctx, n)
    wq3 = w_uq[0].reshape(q_rank, heads, MLA_NOPE + MLA_ROPE)
    wq_rope = wq3[:, :, MLA_NOPE:]
    wq_swap = jnp.stack([_swap_pairs(wq_rope[:, h]) for h in range(heads)], axis=1)
    wq = jnp.concatenate([wq3, wq_swap], axis=2).transpose(1, 0, 2).astype(BF16)
    wkv = w_ukv[0].reshape(kv_rank, heads, MLA_NOPE + MLA_V).transpose(1, 0, 2).astype(BF16)
    tmp = _pick(math.gcd(n, n_ctx), (256, 128, 64, 8))
    qt_mla = _qprep(a_all, n_ctx // tmp, n, g_qn[0][None], wq, cs, heads, q_rank, mla_scale * LOG2_E, tmp)
    k_mla, vt_mla = _kvprep(a_all, g_kvn[0][None], wkv, cs, heads, q_rank, kv_rank, tmp)
    o_a = _mla_attention(qt_mla, k_mla, vt_mla)

    table = _na_bias_table(na_rel_bias[0])
    o_b = _na_attention(q_na, kv_na, table, heads, n_ctx)
    x1 = _matmul_residual(o_a, o_b, w_out_attn[0].astype(BF16), xs, g_m)

    hf, aff = _modulate_router(x1, g_ffn[0][None], sh_f, sc_f, w_router[0])
    moe0 = _ec_moe(hf, aff, w_gate, w_up, w_down, 0)

    m1 = _ada_terms(cvecs, ada_w, ada_b, 1)
    sh_m1, sc_m1, g_m1, sh_f1, sc_f1, g_f1 = [m1[0:1, k * d:(k + 1) * d] for k in range(6)]
    x2, fr = _fourier_transform(x1, g_f, moe0, g_mix[1][None], sh_m1, sc_m1)
    x3 = _matmul_residual(fr, None, w_out_fourier[0].astype(BF16), x2, g_m1)
    hf1, aff1 = _modulate_router(x3, g_ffn[1][None], sh_f1, sc_f1, w_router[1])
    moe1 = _ec_moe(hf1, aff1, w_gate, w_up, w_down, 1)
    out = _final_norm(x3, g_f1, moe1, g_final[None])
    return out[None]
```

```python
import functools
import math

import jax
import jax.numpy as jnp
from jax import lax
from jax.experimental import pallas as pl
from jax.experimental.pallas import tpu as pltpu

F32 = jnp.float32
BF16 = jnp.bfloat16

GRID_W = 64
NORM_EPS = 1e-6
MLA_NOPE = 128
MLA_ROPE = 64
MLA_V = 128
MLA_QK_PAD = 256
MLA_V_PAD = 144
ROPE_THETA = 10000.0
NA_DIM = 128
NA_WIN_H = 8
NA_WIN_W = 16
NA_ROW_BLOCK = 8
NA_KEY_ROWS = 16
FNET_GROUPS = 8
N_EXPERTS = 16
EC_CAPACITY = 2
ROUTER_LANES = 128
COMBINE_BLOCK = 256
COMBINE_WIN = 64
COMBINE_ALIGN = 16
NEG_BIG = -1e30
LOG2_E = math.log2(math.e)

V7X_VMEM_LIMIT_BYTES = 56 * 1024 * 1024


def _cparams(sem, flags=None):
    return pltpu.CompilerParams(dimension_semantics=sem, vmem_limit_bytes=V7X_VMEM_LIMIT_BYTES, flags=flags)


def _pick(n, candidates):
    for c in candidates:
        if n % c == 0:
            return c
    return n


def _ada_kernel(c_ref, w_ref, b_ref, o_ref):
    c = c_ref[...]
    s = c / (1.0 + jnp.exp(-c))
    o_ref[...] = jnp.dot(s, w_ref[...], preferred_element_type=F32,
                         precision=lax.Precision.HIGHEST) + b_ref[...]


def _ada_terms(cvecs, w, b, layer):
    depth, d, n6 = w.shape
    tn = _pick(n6, (1024, 512, 256, 128))
    return pl.pallas_call(
        _ada_kernel,
        grid=(n6 // tn,),
        in_specs=[pl.BlockSpec((8, d), lambda j: (0, 0)),
                  pl.BlockSpec((None, d, tn), lambda j: (layer, 0, j)),
                  pl.BlockSpec((None, 1, tn), lambda j: (layer, 0, j))],
        out_specs=pl.BlockSpec((8, tn), lambda j: (0, j)),
        out_shape=jax.ShapeDtypeStruct((8, n6), F32),
        compiler_params=_cparams(("parallel",)),
        name="ada_terms",
    )(cvecs, w, b.reshape(depth, 1, n6))


def _modulated(x, g, shift, scale):
    ms = jnp.mean(x * x, axis=-1, keepdims=True)
    y = x * lax.rsqrt(ms + NORM_EPS) * g
    return y * (1.0 + scale) + shift


def _mod_kernel(x_ref, g_ref, sh_ref, sc_ref, h_ref):
    h_ref[...] = _modulated(x_ref[...], g_ref[...], sh_ref[...], sc_ref[...]).astype(h_ref.dtype)


def _modulate(x, g, shift, scale, out_dtype=BF16):
    n, d = x.shape
    tm = _pick(n, (512, 256, 128, 64, 8))
    row = pl.BlockSpec((tm, d), lambda i: (i, 0))
    vec = pl.BlockSpec((1, d), lambda i: (0, 0))
    return pl.pallas_call(
        _mod_kernel,
        grid=(n // tm,),
        in_specs=[row, vec, vec, vec],
        out_specs=row,
        out_shape=jax.ShapeDtypeStruct((n, d), out_dtype),
        compiler_params=_cparams(("parallel",)),
        name="modulate",
    )(x, g, shift, scale)


def _router_kernel(n_experts, x_ref, g_ref, sh_ref, sc_ref, wr_ref, h_ref, aff_ref):
    h = _modulated(x_ref[...], g_ref[...], sh_ref[...], sc_ref[...])
    h_ref[...] = h.astype(h_ref.dtype)
    logits = jnp.dot(h, wr_ref[...], preferred_element_type=F32, precision=lax.Precision.HIGHEST)
    lane = lax.broadcasted_iota(jnp.int32, logits.shape, 1)
    logits = jnp.where(lane < n_experts, logits, NEG_BIG)
    m = jnp.max(logits, axis=-1, keepdims=True)
    p = jnp.exp(logits - m)
    aff_ref[...] = p / jnp.sum(p, axis=-1, keepdims=True)


def _modulate_router(x, g, shift, scale, w_router):
    n, d = x.shape
    n_experts = w_router.shape[1]
    wr = jnp.zeros((d, ROUTER_LANES), F32).at[:, :n_experts].set(w_router)
    tm = _pick(n, (512, 256, 128, 64, 8))
    row = pl.BlockSpec((tm, d), lambda i: (i, 0))
    vec = pl.BlockSpec((1, d), lambda i: (0, 0))
    return pl.pallas_call(
        functools.partial(_router_kernel, n_experts),
        grid=(n // tm,),
        in_specs=[row, vec, vec, vec, pl.BlockSpec((d, ROUTER_LANES), lambda i: (0, 0))],
        out_specs=[row, pl.BlockSpec((tm, ROUTER_LANES), lambda i: (i, 0))],
        out_shape=[jax.ShapeDtypeStruct((n, d), BF16),
                   jax.ShapeDtypeStruct((n, ROUTER_LANES), F32)],
        compiler_params=_cparams(("parallel",)),
        name="modulate_router",
    )(x, g, shift, scale, wr)


def _resid_kernel(x_ref, gate_ref, y_ref, o_ref):
    o_ref[...] = x_ref[...] + gate_ref[...] * y_ref[...]


def _residual(x, gate, y):
    n, d = x.shape
    tm = _pick(n, (512, 256, 128, 64, 8))
    row = pl.BlockSpec((tm, d), lambda i: (i, 0))
    vec = pl.BlockSpec((1, d), lambda i: (0, 0))
    return pl.pallas_call(
        _resid_kernel,
        grid=(n // tm,),
        in_specs=[row, vec, row],
        out_specs=row,
        out_shape=jax.ShapeDtypeStruct((n, d), F32),
        compiler_params=_cparams(("parallel",)),
        name="residual",
    )(x, gate, y)


def _final_kernel(x_ref, gate_ref, y_ref, g_ref, o_ref):
    x = x_ref[...] + gate_ref[...] * y_ref[...]
    ms = jnp.mean(x * x, axis=-1, keepdims=True)
    o_ref[...] = x * lax.rsqrt(ms + NORM_EPS) * g_ref[...]


def _final_norm(x, gate, y, g):
    n, d = x.shape
    tm = _pick(n, (512, 256, 128, 64, 8))
    row = pl.BlockSpec((tm, d), lambda i: (i, 0))
    vec = pl.BlockSpec((1, d), lambda i: (0, 0))
    return pl.pallas_call(
        _final_kernel,
        grid=(n // tm,),
        in_specs=[row, vec, row, vec],
        out_specs=row,
        out_shape=jax.ShapeDtypeStruct((n, d), F32),
        compiler_params=_cparams(("parallel",)),
        name="final_norm",
    )(x, gate, y, g)


def _mm_kernel(a_ref, w_ref, cs_ref, o_ref):
    acc = jnp.dot(a_ref[...], w_ref[...], preferred_element_type=F32)
    o_ref[...] = (acc * cs_ref[...]).astype(o_ref.dtype)


def _mm_resid_kernel(a1_ref, a2_ref, w_ref, x_ref, gate_ref, o_ref):
    k1 = a1_ref.shape[1]
    acc = jnp.dot(a1_ref[...], w_ref[:k1, :], preferred_element_type=F32)
    acc = acc + jnp.dot(a2_ref[...], w_ref[k1:, :], preferred_element_type=F32)
    o_ref[...] = x_ref[...] + gate_ref[...] * acc


def _mm_tiles(m, n):
    tm = _pick(m, (1280, 1024, 512, 256, 128, 64, 8))
    tn = n if n <= 1024 else _pick(n, (1024, 512, 256, 128))
    return tm, tn


def _matmul(a, w, col_scale, out_dtype):
    m, k = a.shape
    n = w.shape[1]
    tm, tn = _mm_tiles(m, n)
    return pl.pallas_call(
        _mm_kernel,
        grid=(m // tm, n // tn),
        in_specs=[pl.BlockSpec((tm, k), lambda i, j: (i, 0)),
                  pl.BlockSpec((k, tn), lambda i, j: (0, j)),
                  pl.BlockSpec((1, tn), lambda i, j: (0, j))],
        out_specs=pl.BlockSpec((tm, tn), lambda i, j: (i, j)),
        out_shape=jax.ShapeDtypeStruct((m, n), out_dtype),
        compiler_params=_cparams(("parallel", "parallel")),
        name="matmul",
    )(a, w, col_scale)


def _matmul_residual(a1, a2, w, x, gate):
    m = a1.shape[0]
    n = w.shape[1]
    k1 = w.shape[0] // 2
    second_half = 0
    if a2 is None:
        a2, second_half = a1, 1
    tm, tn = _mm_tiles(m, n)
    return pl.pallas_call(
        _mm_resid_kernel,
        grid=(m // tm, n // tn),
        in_specs=[pl.BlockSpec((tm, k1), lambda i, j: (i, 0)),
                  pl.BlockSpec((tm, k1), lambda i, j: (i, second_half)),
                  pl.BlockSpec((2 * k1, tn), lambda i, j: (0, j)),
                  pl.BlockSpec((tm, tn), lambda i, j: (i, j)),
                  pl.BlockSpec((1, tn), lambda i, j: (0, j))],
        out_specs=pl.BlockSpec((tm, tn), lambda i, j: (i, j)),
        out_shape=jax.ShapeDtypeStruct((m, n), F32),
        compiler_params=_cparams(("parallel", "parallel")),
        name="matmul_residual",
    )(a1, a2, w, x, gate)


def _rms(x, g):
    ms = jnp.mean(x * x, axis=-1, keepdims=True)
    return x * lax.rsqrt(ms + NORM_EPS) * g


def _rope_half(pair, cs):
    t = pair * cs
    t = t + pltpu.roll(t, MLA_ROPE, 1)
    lane = lax.broadcasted_iota(jnp.int32, t.shape, 1)
    return jnp.where(lane < MLA_ROPE, t, 0.0)


def _qprep_kernel(heads, q_rank, scale, a_ref, g_ref, w_ref, cs_ref, q_ref):
    cq = _rms(a_ref[:, :q_rank], g_ref[...]).astype(BF16)
    cs = cs_ref[...]
    for h in range(heads):
        qh = jnp.dot(cq, w_ref[h], preferred_element_type=F32)
        q = jnp.concatenate([qh[:, :MLA_NOPE], _rope_half(qh[:, MLA_NOPE:], cs)], axis=1) * scale
        q_ref[h] = q.T.astype(BF16)


def _kvprep_kernel(heads, q_rank, kv_rank, a_ref, g_ref, w_ref, cs_ref, k_ref, v_ref):
    ckv = _rms(a_ref[:, q_rank:q_rank + kv_rank], g_ref[...]).astype(BF16)
    kr = _rope_half(a_ref[:, q_rank + kv_rank:], cs_ref[...]).astype(BF16)
    row = lax.broadcasted_iota(jnp.int32, (MLA_V_PAD - MLA_V, a_ref.shape[0]), 0)
    ones_rows = jnp.where(row == 0, 1.0, 0.0)
    for h in range(heads):
        kvh = jnp.dot(ckv, w_ref[h], preferred_element_type=F32)
        k_ref[h, :, :MLA_NOPE] = kvh[:, :MLA_NOPE].astype(BF16)
        k_ref[h, :, MLA_NOPE:] = kr
        vt = kvh[:, MLA_NOPE:].T
        v_ref[h] = jnp.concatenate([vt, ones_rows], axis=0).astype(BF16)


def _qprep(a, row_off_blocks, n_rows, g_qn, wq, cs, heads, q_rank, scale, tm):
    aw = a.shape[1]
    return pl.pallas_call(
        functools.partial(_qprep_kernel, heads, q_rank, scale),
        grid=(n_rows // tm,),
        in_specs=[pl.BlockSpec((tm, aw), lambda i: (i + row_off_blocks, 0)),
                  pl.BlockSpec((1, q_rank), lambda i: (0, 0)),
                  pl.BlockSpec((heads, q_rank, MLA_QK_PAD), lambda i: (0, 0, 0)),
                  pl.BlockSpec((tm, 2 * MLA_ROPE), lambda i: (i + row_off_blocks, 0))],
        out_specs=pl.BlockSpec((heads, MLA_QK_PAD, tm), lambda i: (0, 0, i)),
        out_shape=jax.ShapeDtypeStruct((heads, MLA_QK_PAD, n_rows), BF16),
        compiler_params=_cparams(("parallel",)),
        name="mla_q_prep",
    )(a, g_qn, wq, cs)


def _kvprep(a, g_kvn, wkv, cs, heads, q_rank, kv_rank, tm):
    n_rows, aw = a.shape
    return pl.pallas_call(
        functools.partial(_kvprep_kernel, heads, q_rank, kv_rank),
        grid=(n_rows // tm,),
        in_specs=[pl.BlockSpec((tm, aw), lambda i: (i, 0)),
                  pl.BlockSpec((1, kv_rank), lambda i: (0, 0)),
                  pl.BlockSpec((heads, kv_rank, MLA_NOPE + MLA_V), lambda i: (0, 0, 0)),
                  pl.BlockSpec((tm, 2 * MLA_ROPE), lambda i: (i, 0))],
        out_specs=[pl.BlockSpec((heads, tm, MLA_QK_PAD), lambda i: (0, i, 0)),
                   pl.BlockSpec((heads, MLA_V_PAD, tm), lambda i: (0, 0, i))],
        out_shape=[jax.ShapeDtypeStruct((heads, n_rows, MLA_QK_PAD), BF16),
                   jax.ShapeDtypeStruct((heads, MLA_V_PAD, n_rows), BF16)],
        compiler_params=_cparams(("parallel",)),
        name="mla_kv_prep",
    )(a, g_kvn, wkv, cs)


def _dot_nt(a, b):
    return lax.dot_general(a, b, (((1,), (1,)), ((), ())), preferred_element_type=F32)


def _mla_kernel(tk, n_kv, n_sub, qt_ref, k_ref, vt_ref, o_ref, s_ref, p_ref, acc_ref):
    tq = qt_ref.shape[2] // n_sub
    n_chunks = n_kv // tk

    def trip(t, par, carry, do_a=True, do_b=True, do_c=True):
        ms, al_new, al_old = carry
        out_ms, out_al = ms, al_new
        if do_a:
            start = t * tk if isinstance(t, int) else pl.multiple_of(t * tk, tk)
            kc = k_ref[0, pl.ds(start, tk), :]
            out_ms, out_al = [], []
            for u in range(n_sub):
                s = jnp.dot(kc, qt_ref[0, :, u * tq:(u + 1) * tq], preferred_element_type=F32)
                s_ref[par, u] = s
                m_new = jnp.maximum(ms[u], jnp.max(s, axis=0, keepdims=True))
                out_al.append(jnp.exp2(ms[u] - m_new))
                out_ms.append(m_new)
            out_ms, out_al = tuple(out_ms), tuple(out_al)
        if do_b:
            for u in range(n_sub):
                p_ref[1 - par, u] = jnp.exp2((s_ref[1 - par, u] - ms[u]).astype(BF16))
        if do_c:
            start = (t - 2) * tk if isinstance(t, int) else pl.multiple_of((t - 2) * tk, tk)
            vc = vt_ref[0, :, pl.ds(start, tk)]
            for u in range(n_sub):
                acc_ref[u] = al_old[u] * acc_ref[u] + jnp.dot(vc, p_ref[par, u], preferred_element_type=F32)
        return out_ms, out_al, al_new

    acc_ref[...] = jnp.zeros_like(acc_ref)
    neg = tuple(jnp.full((1, tq), NEG_BIG, F32) for _ in range(n_sub))
    zero = tuple(jnp.zeros((1, tq), F32) for _ in range(n_sub))
    carry = (neg, zero, zero)
    carry = trip(0, 0, carry, do_b=False, do_c=False)
    if n_chunks > 1:
        carry = trip(1, 1, carry, do_c=False)
    n_steady = max(n_chunks - 2, 0)

    def double_trip(i, carry):
        t = 2 + 2 * i
        return trip(t + 1, 1, trip(t, 0, carry))

    carry = lax.fori_loop(0, n_steady // 2, double_trip, carry)
    if n_steady % 2 == 1:
        carry = trip(n_chunks - 1, (n_chunks - 1) % 2, carry)
    carry = trip(n_chunks, n_chunks % 2, carry, do_a=False, do_c=n_chunks > 1)
    if n_chunks > 1:
        carry = trip(n_chunks + 1, (n_chunks + 1) % 2, carry, do_a=False, do_b=False)
    else:
        carry = trip(2, 0, carry, do_a=False, do_b=False)
    for u in range(n_sub):
        o = acc_ref[u, :MLA_V, :] / acc_ref[u, MLA_V:MLA_V + 1, :]
        o_ref[u * tq:(u + 1) * tq, :] = o.T.astype(o_ref.dtype)


def _mla_attention(qt, k, vt):
    heads, _, n = qt.shape
    n_kv = k.shape[1]
    tq = _pick(n, (256, 128))
    n_sub = 4 if n % (4 * tq) == 0 else 1
    tk = _pick(n_kv, (1280, 768, 512, 256, 128))
    return pl.pallas_call(
        functools.partial(_mla_kernel, tk, n_kv, n_sub),
        grid=(heads, n // (tq * n_sub)),
        in_specs=[pl.BlockSpec((1, MLA_QK_PAD, tq * n_sub), lambda h, i: (h, 0, i)),
                  pl.BlockSpec((1, n_kv, MLA_QK_PAD), lambda h, i: (h, 0, 0)),
                  pl.BlockSpec((1, MLA_V_PAD, n_kv), lambda h, i: (h, 0, 0))],
        out_specs=pl.BlockSpec((tq * n_sub, MLA_V), lambda h, i: (i, h)),
        out_shape=jax.ShapeDtypeStruct((n, heads * MLA_V), BF16),
        scratch_shapes=[pltpu.VMEM((2, n_sub, tk, tq), F32),
                        pltpu.VMEM((2, n_sub, tk, tq), BF16),
                        pltpu.VMEM((n_sub, MLA_V_PAD, tq), F32)],
        compiler_params=_cparams(("parallel", "parallel")),
        name="mla_attention",
    )(qt, k, vt)


def _na_kernel(n_ctx, rows_total, win_h, key_rows, q_ref, k_ref, v_ref, t_ref, o_ref, bm_ref):
    b = pl.program_id(1)
    nb = pl.num_programs(1)
    r0 = b * NA_ROW_BLOCK
    base = jnp.clip(r0 - NA_WIN_H // 2, 0, rows_total - key_rows)

    @pl.when((b <= 1) | (b == nb - 1))
    def _():
        for i in range(NA_ROW_BLOCK):
            rs = jnp.clip(r0 + i - win_h // 2, 0, rows_total - win_h)
            for j in range(key_rows):
                kr = base + j
                valid = (kr >= rs) & (kr < rs + win_h)
                idx = jnp.clip(kr - (r0 + i) + NA_WIN_H - 1, 0, 2 * NA_WIN_H - 2)
                bm_ref[i * GRID_W:(i + 1) * GRID_W, j * GRID_W:(j + 1) * GRID_W] = jnp.where(
                    valid, t_ref[0, idx], NEG_BIG)

    q = q_ref[...]
    n_loc = key_rows * GRID_W
    loc0 = pl.multiple_of(n_ctx + base * GRID_W, GRID_W)
    s_ctx = _dot_nt(q, k_ref[0:n_ctx, :])
    s_loc = _dot_nt(q, k_ref[pl.ds(loc0, n_loc), :]) + bm_ref[...]
    m = jnp.maximum(jnp.max(s_ctx, axis=-1, keepdims=True), jnp.max(s_loc, axis=-1, keepdims=True))
    p_ctx = jnp.exp(s_ctx - m)
    p_loc = jnp.exp(s_loc - m)
    l = jnp.sum(p_ctx, axis=-1, keepdims=True) + jnp.sum(p_loc, axis=-1, keepdims=True)
    o = jnp.dot(p_ctx.astype(BF16), v_ref[0:n_ctx, :], preferred_element_type=F32)
    o = o + jnp.dot(p_loc.astype(BF16), v_ref[pl.ds(loc0, n_loc), :], preferred_element_type=F32)
    o_ref[...] = (o / l).astype(o_ref.dtype)


def _na_bias_table(rel_bias):
    cq = jnp.arange(GRID_W)
    ck = jnp.arange(GRID_W)
    col_start = jnp.clip(cq - NA_WIN_W // 2, 0, GRID_W - NA_WIN_W)
    valid = (ck[None, :] >= col_start[:, None]) & (ck[None, :] < col_start[:, None] + NA_WIN_W)
    co = jnp.clip(ck[None, :] - cq[:, None] + NA_WIN_W - 1, 0, 2 * NA_WIN_W - 2)
    onehot = (co[None] == jnp.arange(2 * NA_WIN_W - 1)[:, None, None]).astype(F32)
    t = jnp.einsum('hrc,cqk->hrqk', rel_bias, onehot, precision=lax.Precision.HIGHEST)
    return jnp.where(valid[None, None], t, NEG_BIG)


def _na_attention(q, kv, table, heads, n_ctx):
    n = q.shape[0]
    n_kv = kv.shape[0]
    rows_total = n // GRID_W
    win_h = min(NA_WIN_H, rows_total)
    key_rows = min(NA_KEY_ROWS, rows_total)
    tq = NA_ROW_BLOCK * GRID_W
    n_off = table.shape[1]
    return pl.pallas_call(
        functools.partial(_na_kernel, n_ctx, rows_total, win_h, key_rows),
        grid=(heads, n // tq),
        in_specs=[pl.BlockSpec((tq, NA_DIM), lambda h, i: (i, h)),
                  pl.BlockSpec((n_kv, NA_DIM), lambda h, i: (0, h)),
                  pl.BlockSpec((n_kv, NA_DIM), lambda h, i: (0, heads + h)),
                  pl.BlockSpec((1, n_off, GRID_W, GRID_W), lambda h, i: (h, 0, 0, 0))],
        out_specs=pl.BlockSpec((tq, NA_DIM), lambda h, i: (i, h)),
        out_shape=jax.ShapeDtypeStruct((n, heads * NA_DIM), BF16),
        scratch_shapes=[pltpu.VMEM((tq, key_rows * GRID_W), F32)],
        compiler_params=_cparams(("parallel", "arbitrary")),
        name="na_attention",
    )(q, kv, kv, table)


def _fft_a_kernel(groups, x_ref, gate_ref, y_ref, g_ref, sh_ref, sc_ref, w_ref, xo_ref, zr_ref, zi_ref):
    x = x_ref[...] + gate_ref[...] * y_ref[...]
    xo_ref[...] = x
    h = _modulated(x, g_ref[...], sh_ref[...], sc_ref[...]).astype(BF16)
    dg = h.shape[1] // groups
    for g in range(groups):
        z = jnp.dot(h[:, g * dg:(g + 1) * dg], w_ref[...], preferred_element_type=F32)
        zr_ref[:, g * dg:(g + 1) * dg] = z[:, :dg].astype(BF16)
        zi_ref[:, g * dg:(g + 1) * dg] = z[:, dg:].astype(BF16)


def _fft_b_kernel(zr_ref, zi_ref, w_ref, yr_ref, yi_ref):
    n1 = zr_ref.shape[0]
    z = jnp.concatenate([zr_ref[...], zi_ref[...]], axis=0)
    y = jnp.dot(w_ref[...], z, preferred_element_type=F32)
    yr_ref[...] = y[:n1].astype(BF16)
    yi_ref[...] = y[n1:].astype(BF16)


def _fft_c_kernel(yr_ref, yi_ref, w_ref, o_ref):
    y = jnp.concatenate([yr_ref[0], yi_ref[0]], axis=0)
    o_ref[...] = jnp.dot(w_ref[0], y, preferred_element_type=F32).astype(o_ref.dtype)


def _dft_tables(n, dg):
    lg = int(math.log2(n))
    n1 = 1 << ((lg + 1) // 2)
    n2 = n // n1
    scale_a = 1.0 / math.sqrt(dg)
    sb = 1 << (int(math.log2(n1)) // 2)
    scale_b = 1.0 / sb
    scale_c = 1.0 / (math.sqrt(n) / sb)
    ch = jnp.arange(dg, dtype=jnp.int32)
    ang = (2.0 * math.pi / dg) * ((ch[:, None] * ch[None, :]) % dg).astype(F32)
    wa = jnp.concatenate([jnp.cos(ang), -jnp.sin(ang)], axis=1) * scale_a
    a = jnp.arange(n1, dtype=jnp.int32)
    ang = (2.0 * math.pi / n1) * ((a[:, None] * a[None, :]) % n1).astype(F32)
    c, s = jnp.cos(ang), jnp.sin(ang)
    wb = jnp.concatenate([jnp.concatenate([c, s], axis=1), jnp.concatenate([-s, c], axis=1)], axis=0) * scale_b
    k1 = jnp.arange(n1, dtype=jnp.int32)[:, None, None]
    k2 = jnp.arange(n2, dtype=jnp.int32)[None, :, None]
    b = jnp.arange(n2, dtype=jnp.int32)[None, None, :]
    ang = (2.0 * math.pi / n) * ((b * (k1 + n1 * k2)) % n).astype(F32)
    wc = jnp.concatenate([jnp.cos(ang), jnp.sin(ang)], axis=2) * scale_c
    return n1, n2, wa.astype(BF16), wb.astype(BF16), wc.astype(BF16)


def _fourier_transform(x, gate, y, g, shift, scale):
    n, d = x.shape
    dg = d // FNET_GROUPS
    n1, n2, wa, wb, wc = _dft_tables(n, dg)
    tm = _pick(n, (512, 256, 128, 64, 8))
    row = pl.BlockSpec((tm, d), lambda i: (i, 0))
    vec = pl.BlockSpec((1, d), lambda i: (0, 0))
    x_new, zr, zi = pl.pallas_call(
        functools.partial(_fft_a_kernel, FNET_GROUPS),
        grid=(n // tm,),
        in_specs=[row, vec, row, vec, vec, vec, pl.BlockSpec((dg, 2 * dg), lambda i: (0, 0))],
        out_specs=[row, row, row],
        out_shape=[jax.ShapeDtypeStruct((n, d), F32),
                   jax.ShapeDtypeStruct((n, d), BF16),
                   jax.ShapeDtypeStruct((n, d), BF16)],
        compiler_params=_cparams(("parallel",)),
        name="fft_channels",
    )(x, gate, y, g, shift, scale, wa)

    cols = n2 * d
    tn = _pick(cols, (4096, 2048, 1024, 512, 256, 128))
    col = pl.BlockSpec((n1, tn), lambda j: (0, j))
    yr, yi = pl.pallas_call(
        _fft_b_kernel,
        grid=(cols // tn,),
        in_specs=[col, col, pl.BlockSpec((2 * n1, 2 * n1), lambda j: (0, 0))],
        out_specs=[col, col],
        out_shape=[jax.ShapeDtypeStruct((n1, cols), BF16)] * 2,
        compiler_params=_cparams(("parallel",)),
        name="fft_pos_outer",
    )(zr.reshape(n1, cols), zi.reshape(n1, cols), wb)

    slab = pl.BlockSpec((1, n2, d), lambda k: (k, 0, 0))
    f = pl.pallas_call(
        _fft_c_kernel,
        grid=(n1,),
        in_specs=[slab, slab, pl.BlockSpec((1, n2, 2 * n2), lambda k: (k, 0, 0))],
        out_specs=pl.BlockSpec((n2, d), lambda k: (0, k)),
        out_shape=jax.ShapeDtypeStruct((n2, n1 * d), BF16),
        compiler_params=_cparams(("parallel",)),
        name="fft_pos_inner",
    )(yr.reshape(n1, n2, d), yi.reshape(n1, n2, d), wc)
    return x_new, f.reshape(n, d)


def _expert_kernel(x_ref, wg_ref, wu_ref, wd_ref, gate_ref, o_ref, acc_ref):
    f = pl.program_id(2)

    @pl.when(f == 0)
    def _():
        acc_ref[...] = jnp.zeros_like(acc_ref)

    x = x_ref[0]
    a = jnp.dot(x, wg_ref[...].astype(BF16), preferred_element_type=F32)
    u = jnp.dot(x, wu_ref[...].astype(BF16), preferred_element_type=F32)
    hmid = (a / (1.0 + jnp.exp(-a)) * u).astype(BF16)
    acc_ref[...] += jnp.dot(hmid, wd_ref[...].astype(BF16), preferred_element_type=F32)

    @pl.when(f == pl.num_programs(2) - 1)
    def _():
        o_ref[0] = (acc_ref[...] * gate_ref[0]).astype(o_ref.dtype)


def _expert_ffn(xs, w_gate, w_up, w_down, gates, layer):
    e, cap, d = xs.shape
    ff = w_gate.shape[3]
    tm = _pick(cap, (1024, 512, 256, 128, 64, 8))
    tf = _pick(ff, (256, 128))
    return pl.pallas_call(
        _expert_kernel,
        grid=(e, cap // tm, ff // tf),
        in_specs=[pl.BlockSpec((1, tm, d), lambda ei, i, f: (ei, i, 0)),
                  pl.BlockSpec((None, None, d, tf), lambda ei, i, f: (layer, ei, 0, f)),
                  pl.BlockSpec((None, None, d, tf), lambda ei, i, f: (layer, ei, 0, f)),
                  pl.BlockSpec((None, None, tf, d), lambda ei, i, f: (layer, ei, f, 0)),
                  pl.BlockSpec((1, tm, 1), lambda ei, i, f: (ei, i, 0))],
        out_specs=pl.BlockSpec((1, tm, d), lambda ei, i, f: (ei, i, 0)),
        out_shape=jax.ShapeDtypeStruct((e, cap, d), BF16),
        scratch_shapes=[pltpu.VMEM((tm, d), F32)],
        compiler_params=_cparams(("parallel", "parallel", "arbitrary")),
        name="expert_ffn",
    )(xs, w_gate, w_up, w_down, gates)


def _combine_kernel(n_exp, cap, w0_ref, npass_ref, slot_ref, w0v_ref, y_hbm, r_ref, o_ref, buf_ref, sem):
    b = pl.program_id(0)
    win = COMBINE_WIN
    o_ref[...] = jnp.zeros_like(o_ref)
    slot = slot_ref[...]
    w0v = w0v_ref[...]
    jio = (lax.broadcasted_iota(jnp.int32, (1, n_exp * win), 1) % win).astype(F32)

    def window_copy(e, start):
        return pltpu.make_async_copy(y_hbm.at[e, pl.ds(start, win), :],
                                     buf_ref.at[pl.ds(e * win, win), :], sem.at[e])

    def one_pass(p, carry):
        starts = []
        for e in range(n_exp):
            st = jnp.minimum(w0_ref[b * n_exp + e] + p * win, cap - win)
            starts.append(pl.multiple_of(st, COMBINE_ALIGN))
            window_copy(e, starts[e]).start()
        lo = w0v + p * win
        st_vec = jnp.minimum(lo, cap - win)
        valid = (slot >= lo) & (slot < lo + win)
        rel = jnp.where(valid, slot - st_vec, -1).astype(F32).astype(BF16)
        relx = jnp.dot(rel, r_ref[...], preferred_element_type=F32)
        onehot = jnp.where(relx == jio, 1.0, 0.0).astype(BF16)
        for e in range(n_exp):
            window_copy(e, starts[e]).wait()
        o_ref[...] += jnp.dot(onehot, buf_ref[...], preferred_element_type=F32)
        return carry

    lax.fori_loop(0, npass_ref[b], one_pass, 0)


def _combine(y, slot, w0, npass, n):
    n_exp, cap, d = y.shape
    tb = _pick(n, (COMBINE_BLOCK,))
    nb = n // tb
    win = COMBINE_WIN
    rows = lax.broadcasted_iota(jnp.int32, (ROUTER_LANES, n_exp * win), 0)
    cols = lax.broadcasted_iota(jnp.int32, (ROUTER_LANES, n_exp * win), 1)
    spread = (cols // win == rows).astype(BF16)
    w0v = jnp.zeros((nb, 1, ROUTER_LANES), jnp.int32).at[:, 0, :n_exp].set(w0)
    grid_spec = pltpu.PrefetchScalarGridSpec(
        num_scalar_prefetch=2,
        grid=(nb,),
        in_specs=[pl.BlockSpec((tb, ROUTER_LANES), lambda b, *_: (b, 0)),
                  pl.BlockSpec((None, 1, ROUTER_LANES), lambda b, *_: (b, 0, 0)),
                  pl.BlockSpec(memory_space=pl.ANY),
                  pl.BlockSpec((ROUTER_LANES, n_exp * win), lambda b, *_: (0, 0))],
        out_specs=pl.BlockSpec((tb, d), lambda b, *_: (b, 0)),
        scratch_shapes=[pltpu.VMEM((n_exp * win, d), BF16),
                        pltpu.SemaphoreType.DMA((n_exp,))])
    return pl.pallas_call(
        functools.partial(_combine_kernel, n_exp, cap),
        grid_spec=grid_spec,
        out_shape=jax.ShapeDtypeStruct((n, d), F32),
        compiler_params=_cparams(("arbitrary",)),
        name="moe_combine",
    )(w0.reshape(-1), npass, slot, w0v, y, spread)


def _ec_moe(hf, aff, w_gate, w_up, w_down, layer):
    n, d = hf.shape
    e = w_gate.shape[1]
    cap = EC_CAPACITY * n // e
    aff_e = aff[:, :e]
    gates, idx = lax.top_k(aff_e.T, cap)
    idx_s, gates_s = lax.sort((idx, gates), dimension=1, num_keys=1)
    xs = jnp.take(hf, idx_s.reshape(-1), axis=0).reshape(e, cap, d)
    y = _expert_ffn(xs, w_gate, w_up, w_down, gates_s[..., None], layer)

    thr = gates[:, cap - 1][None, :]
    gt = aff_e > thr
    eq = aff_e == thr
    need = cap - jnp.sum(gt, axis=0, keepdims=True)
    eq_i = eq.astype(jnp.int32)
    sel = gt | (eq & (jnp.cumsum(eq_i, axis=0) - eq_i < need))
    sel_i = sel.astype(jnp.int32)
    pos = jnp.cumsum(sel_i, axis=0) - sel_i
    slot = jnp.full((n, ROUTER_LANES), -1, jnp.int32).at[:, :e].set(jnp.where(sel, pos, -1))
    tb = _pick(n, (COMBINE_BLOCK,))
    off = pos[::tb]
    cnt = jnp.sum(sel_i.reshape(n // tb, tb, e), axis=1)
    w0 = jnp.minimum(off // COMBINE_ALIGN * COMBINE_ALIGN, cap - COMBINE_WIN)
    npass = jnp.max(jnp.where(cnt > 0, (off + cnt - w0 + COMBINE_WIN - 1) // COMBINE_WIN, 0), axis=1)
    return _combine(y, slot, w0.astype(jnp.int32), npass.astype(jnp.int32), n)


def _rope_table(n_ctx, n):
    t = jnp.arange(n)
    n_freq = MLA_ROPE // 4
    inv_freq = ROPE_THETA ** (-jnp.arange(n_freq, dtype=F32) / n_freq)
    ar = (t // GRID_W).astype(F32)[:, None] * inv_freq
    ac = (t % GRID_W).astype(F32)[:, None] * inv_freq
    cos = jnp.concatenate([jnp.cos(ar), jnp.cos(ar), jnp.cos(ac), jnp.cos(ac)], axis=1)
    sin = jnp.concatenate([-jnp.sin(ar), jnp.sin(ar), -jnp.sin(ac), jnp.sin(ac)], axis=1)
    tab = jnp.concatenate([cos, sin], axis=1)
    ident = jnp.concatenate([jnp.ones((n_ctx, MLA_ROPE), F32), jnp.zeros((n_ctx, MLA_ROPE), F32)], axis=1)
    return jnp.concatenate([ident, tab], axis=0)


def _swap_pairs(w):
    q = MLA_ROPE // 4
    return jnp.concatenate([w[:, q:2 * q], w[:, :q], w[:, 3 * q:], w[:, 2 * q:3 * q]], axis=1)


def kernel(x, c, ctx, c_ctx, ada_w, ada_b, g_mix, g_ffn, g_final, w_in, g_qn, w_uq, g_kvn, w_ukv,
           na_rel_bias, w_out_attn, w_out_fourier, w_router, w_gate, w_up, w_down):
    _, n, d = x.shape
    n_ctx = ctx.shape[1]
    heads = d // 256
    q_rank = d // 4
    kv_rank = d // 8
    na_width = heads * NA_DIM
    mla_scale = (MLA_NOPE + MLA_ROPE) ** -0.5
    na_scale = NA_DIM ** -0.5
    xs = x[0]
    cx = ctx[0]

    cvecs = jnp.zeros((8, d), F32).at[0].set(c[0]).at[1].set(c_ctx)

    m0 = _ada_terms(cvecs, ada_w, ada_b, 0)
    sh_m, sc_m, g_m, sh_f, sc_f, g_f = [m0[0:1, k * d:(k + 1) * d] for k in range(6)]
    csh_m, csc_m = m0[1:2, 0:d], m0[1:2, d:2 * d]
    gm0 = g_mix[0][None]
    h_x = _modulate(xs, gm0, sh_m, sc_m)
    h_c = _modulate(cx, gm0, csh_m, csc_m)
    h_all = jnp.concatenate([h_c, h_x], axis=0)

    w0 = w_in[0]
    o1, o2, o3 = q_rank, q_rank + kv_rank, q_rank + kv_rank + MLA_ROPE
    w_kr = w0[:, o2:o3]
    w_a = jnp.concatenate([w0[:, :o2], w_kr, _swap_pairs(w_kr)], axis=1).astype(BF16)
    w_q = w0[:, o3:o3 + na_width].astype(BF16)
    w_kv = w0[:, o3 + na_width:].astype(BF16)
    ones = lambda k: jnp.ones((1, k), F32)
    a_all = _matmul(h_all, w_a, ones(w_a.shape[1]), F32)
    q_na = _matmul(h_x, w_q, jnp.full((1, na_width), na_scale, F32), BF16)
    kv_na = _matmul(h_all, w_kv, ones(2 * na_width), BF16)

    cs = _rope_table(n_ctx, n)
    wq3 = w_uq[0].reshape(q_rank, heads, MLA_NOPE + MLA_ROPE)
    wq_rope = wq3[:, :, MLA_NOPE:]
    wq_swap = jnp.stack([_swap_pairs(wq_rope[:, h]) for h in range(heads)], axis=1)
    wq = jnp.concatenate([wq3, wq_swap], axis=2).transpose(1, 0, 2).astype(BF16)
    wkv = w_ukv[0].reshape(kv_rank, heads, MLA_NOPE + MLA_V).transpose(1, 0, 2).astype(BF16)
    tmp = _pick(math.gcd(n, n_ctx), (256, 128, 64, 8))
    qt_mla = _qprep(a_all, n_ctx // tmp, n, g_qn[0][None], wq, cs, heads, q_rank, mla_scale * LOG2_E, tmp)
    k_mla, vt_mla = _kvprep(a_all, g_kvn[0][None], wkv, cs, heads, q_rank, kv_rank, tmp)
    o_a = _mla_attention(qt_mla, k_mla, vt_mla)

    table = _na_bias_table(na_rel_bias[0])
    o_b = _na_attention(q_na, kv_na, table, heads, n_ctx)
    x1 = _matmul_residual(o_a, o_b, w_out_attn[0].astype(BF16), xs, g_m)

    hf, aff = _modulate_router(x1, g_ffn[0][None], sh_f, sc_f, w_router[0])
    moe0 = _ec_moe(hf, aff, w_gate, w_up, w_down, 0)

    m1 = _ada_terms(cvecs, ada_w, ada_b, 1)
    sh_m1, sc_m1, g_m1, sh_f1, sc_f1, g_f1 = [m1[0:1, k * d:(k + 1) * d] for k in range(6)]
    x2, fr = _fourier_transform(x1, g_f, moe0, g_mix[1][None], sh_m1, sc_m1)
    x3 = _matmul_residual(fr, None, w_out_fourier[0].astype(BF16), x2, g_m1)
    hf1, aff1 = _modulate_router(x3, g_ffn[1][None], sh_f1, sc_f1, w_router[1])
    moe1 = _ec_moe(hf1, aff1, w_gate, w_up, w_down, 1)
    out = _final_norm(x3, g_f1, moe1, g_final[None])
    return out[None]
```

```python
import functools
import math

import jax
import jax.numpy as jnp
from jax import lax
from jax.experimental import pallas as pl
from jax.experimental.pallas import tpu as pltpu

F32 = jnp.float32
BF16 = jnp.bfloat16

GRID_W = 64
NORM_EPS = 1e-6
MLA_NOPE = 128
MLA_ROPE = 64
MLA_V = 128
MLA_QK_PAD = 256
MLA_V_PAD = 144
ROPE_THETA = 10000.0
NA_DIM = 128
NA_WIN_H = 8
NA_WIN_W = 16
NA_ROW_BLOCK = 8
NA_KEY_ROWS = 16
FNET_GROUPS = 8
FFT_ROWS_PER_STEP = 8
N_EXPERTS = 16
EC_CAPACITY = 2
ROUTER_LANES = 128
COMBINE_BLOCK = 256
COMBINE_WIN = 64
COMBINE_ALIGN = 16
NEG_BIG = -1e30
LOG2_E = math.log2(math.e)

V7X_VMEM_LIMIT_BYTES = 56 * 1024 * 1024


def _cparams(sem, flags=None):
    return pltpu.CompilerParams(dimension_semantics=sem, vmem_limit_bytes=V7X_VMEM_LIMIT_BYTES, flags=flags)


def _pick(n, candidates):
    for c in candidates:
        if n % c == 0:
            return c
    return n


def _ada_kernel(c_ref, w_ref, b_ref, o_ref):
    c = c_ref[...]
    s = c / (1.0 + jnp.exp(-c))
    o_ref[...] = jnp.dot(s, w_ref[...], preferred_element_type=F32,
                         precision=lax.Precision.HIGHEST) + b_ref[...]


def _ada_terms(cvecs, w, b, layer):
    depth, d, n6 = w.shape
    tn = _pick(n6, (1024, 512, 256, 128))
    return pl.pallas_call(
        _ada_kernel,
        grid=(n6 // tn,),
        in_specs=[pl.BlockSpec((8, d), lambda j: (0, 0)),
                  pl.BlockSpec((None, d, tn), lambda j: (layer, 0, j)),
                  pl.BlockSpec((None, 1, tn), lambda j: (layer, 0, j))],
        out_specs=pl.BlockSpec((8, tn), lambda j: (0, j)),
        out_shape=jax.ShapeDtypeStruct((8, n6), F32),
        compiler_params=_cparams(("parallel",)),
        name="ada_terms",
    )(cvecs, w, b.reshape(depth, 1, n6))


def _modulated(x, g, shift, scale):
    ms = jnp.mean(x * x, axis=-1, keepdims=True)
    y = x * lax.rsqrt(ms + NORM_EPS) * g
    return y * (1.0 + scale) + shift


def _mod_kernel(x_ref, g_ref, sh_ref, sc_ref, h_ref):
    h_ref[...] = _modulated(x_ref[...], g_ref[...], sh_ref[...], sc_ref[...]).astype(h_ref.dtype)


def _modulate(x, g, shift, scale, out_dtype=BF16):
    n, d = x.shape
    tm = _pick(n, (512, 256, 128, 64, 8))
    row = pl.BlockSpec((tm, d), lambda i: (i, 0))
    vec = pl.BlockSpec((1, d), lambda i: (0, 0))
    return pl.pallas_call(
        _mod_kernel,
        grid=(n // tm,),
        in_specs=[row, vec, vec, vec],
        out_specs=row,
        out_shape=jax.ShapeDtypeStruct((n, d), out_dtype),
        compiler_params=_cparams(("parallel",)),
        name="modulate",
    )(x, g, shift, scale)


def _router_kernel(n_experts, x_ref, g_ref, sh_ref, sc_ref, whi_ref, wlo_ref, h_ref, aff_ref):
    h = _modulated(x_ref[...], g_ref[...], sh_ref[...], sc_ref[...])
    h_hi = h.astype(BF16)
    h_ref[...] = h_hi
    h_lo = (h - h_hi.astype(F32)).astype(BF16)
    logits = (jnp.dot(h_hi, whi_ref[...], preferred_element_type=F32)
              + jnp.dot(h_hi, wlo_ref[...], preferred_element_type=F32)
              + jnp.dot(h_lo, whi_ref[...], preferred_element_type=F32))
    lane = lax.broadcasted_iota(jnp.int32, logits.shape, 1)
    logits = jnp.where(lane < n_experts, logits, NEG_BIG)
    m = jnp.max(logits, axis=-1, keepdims=True)
    p = jnp.exp(logits - m)
    aff_ref[...] = p / jnp.sum(p, axis=-1, keepdims=True)


def _modulate_router(x, g, shift, scale, w_router):
    n, d = x.shape
    n_experts = w_router.shape[1]
    wr = jnp.zeros((d, ROUTER_LANES), F32).at[:, :n_experts].set(w_router)
    wr_hi = wr.astype(BF16)
    wr_lo = (wr - wr_hi.astype(F32)).astype(BF16)
    tm = _pick(n, (512, 256, 128, 64, 8))
    row = pl.BlockSpec((tm, d), lambda i: (i, 0))
    vec = pl.BlockSpec((1, d), lambda i: (0, 0))
    wspec = pl.BlockSpec((d, ROUTER_LANES), lambda i: (0, 0))
    return pl.pallas_call(
        functools.partial(_router_kernel, n_experts),
        grid=(n // tm,),
        in_specs=[row, vec, vec, vec, wspec, wspec],
        out_specs=[row, pl.BlockSpec((tm, ROUTER_LANES), lambda i: (i, 0))],
        out_shape=[jax.ShapeDtypeStruct((n, d), BF16),
                   jax.ShapeDtypeStruct((n, ROUTER_LANES), F32)],
        compiler_params=_cparams(("parallel",)),
        name="modulate_router",
    )(x, g, shift, scale, wr_hi, wr_lo)


def _resid_kernel(x_ref, gate_ref, y_ref, o_ref):
    o_ref[...] = x_ref[...] + gate_ref[...] * y_ref[...]


def _residual(x, gate, y):
    n, d = x.shape
    tm = _pick(n, (512, 256, 128, 64, 8))
    row = pl.BlockSpec((tm, d), lambda i: (i, 0))
    vec = pl.BlockSpec((1, d), lambda i: (0, 0))
    return pl.pallas_call(
        _resid_kernel,
        grid=(n // tm,),
        in_specs=[row, vec, row],
        out_specs=row,
        out_shape=jax.ShapeDtypeStruct((n, d), F32),
        compiler_params=_cparams(("parallel",)),
        name="residual",
    )(x, gate, y)


def _final_kernel(x_ref, gate_ref, y_ref, g_ref, o_ref):
    x = x_ref[...] + gate_ref[...] * y_ref[...]
    ms = jnp.mean(x * x, axis=-1, keepdims=True)
    o_ref[...] = x * lax.rsqrt(ms + NORM_EPS) * g_ref[...]


def _final_norm(x, gate, y, g):
    n, d = x.shape
    tm = _pick(n, (512, 256, 128, 64, 8))
    row = pl.BlockSpec((tm, d), lambda i: (i, 0))
    vec = pl.BlockSpec((1, d), lambda i: (0, 0))
    return pl.pallas_call(
        _final_kernel,
        grid=(n // tm,),
        in_specs=[row, vec, row, vec],
        out_specs=row,
        out_shape=jax.ShapeDtypeStruct((n, d), F32),
        compiler_params=_cparams(("parallel",)),
        name="final_norm",
    )(x, gate, y, g)


def _mm_kernel(a_ref, w_ref, cs_ref, o_ref):
    acc = jnp.dot(a_ref[...], w_ref[...], preferred_element_type=F32)
    o_ref[...] = (acc * cs_ref[...]).astype(o_ref.dtype)


def _mm_resid_kernel(a1_ref, a2_ref, w_ref, x_ref, gate_ref, o_ref):
    k1 = a1_ref.shape[1]
    acc = jnp.dot(a1_ref[...], w_ref[:k1, :], preferred_element_type=F32)
    acc = acc + jnp.dot(a2_ref[...], w_ref[k1:, :], preferred_element_type=F32)
    o_ref[...] = x_ref[...] + gate_ref[...] * acc


def _mm_resid_packed_kernel(ap_ref, w_ref, x_ref, gate_ref, o_ref):
    k1 = ap_ref.shape[1]
    a1, a2 = _unpack_pair(ap_ref[...])
    acc = jnp.dot(a1, w_ref[:k1, :], preferred_element_type=F32)
    acc = acc + jnp.dot(a2, w_ref[k1:, :], preferred_element_type=F32)
    o_ref[...] = x_ref[...] + gate_ref[...] * acc


def _mm_tiles(m, n):
    tm = _pick(m, (1280, 1024, 512, 256, 128, 64, 8))
    tn = n if n <= 1024 else _pick(n, (1024, 512, 256, 128))
    return tm, tn


def _matmul(a, w, col_scale, out_dtype):
    m, k = a.shape
    n = w.shape[1]
    tm, tn = _mm_tiles(m, n)
    return pl.pallas_call(
        _mm_kernel,
        grid=(m // tm, n // tn),
        in_specs=[pl.BlockSpec((tm, k), lambda i, j: (i, 0)),
                  pl.BlockSpec((k, tn), lambda i, j: (0, j)),
                  pl.BlockSpec((1, tn), lambda i, j: (0, j))],
        out_specs=pl.BlockSpec((tm, tn), lambda i, j: (i, j)),
        out_shape=jax.ShapeDtypeStruct((m, n), out_dtype),
        compiler_params=_cparams(("parallel", "parallel")),
        name="matmul",
    )(a, w, col_scale)


def _matmul_residual(a1, a2, w, x, gate):
    m = a1.shape[0]
    n = w.shape[1]
    k1 = w.shape[0] // 2
    tm, tn = _mm_tiles(m, n)
    a_spec = pl.BlockSpec((tm, k1), lambda i, j: (i, 0))
    operands, a_specs, body = (a1, a2), [a_spec, a_spec], _mm_resid_kernel
    if a2 is None:
        operands, a_specs, body = (a1,), [a_spec], _mm_resid_packed_kernel
    return pl.pallas_call(
        body,
        grid=(m // tm, n // tn),
        in_specs=a_specs + [pl.BlockSpec((2 * k1, tn), lambda i, j: (0, j)),
                            pl.BlockSpec((tm, tn), lambda i, j: (i, j)),
                            pl.BlockSpec((1, tn), lambda i, j: (0, j))],
        out_specs=pl.BlockSpec((tm, tn), lambda i, j: (i, j)),
        out_shape=jax.ShapeDtypeStruct((m, n), F32),
        compiler_params=_cparams(("parallel", "parallel")),
        name="matmul_residual",
    )(*operands, w, x, gate)


def _rms(x, g):
    ms = jnp.mean(x * x, axis=-1, keepdims=True)
    return x * lax.rsqrt(ms + NORM_EPS) * g


def _rope_half(pair, cs):
    t = pair * cs
    t = t + pltpu.roll(t, MLA_ROPE, 1)
    lane = lax.broadcasted_iota(jnp.int32, t.shape, 1)
    return jnp.where(lane < MLA_ROPE, t, 0.0)


def _qprep_kernel(heads, q_rank, scale, a_ref, g_ref, w_ref, cs_ref, q_ref):
    cq = _rms(a_ref[:, :q_rank], g_ref[...]).astype(BF16)
    cs = cs_ref[...]
    for h in range(heads):
        qh = jnp.dot(cq, w_ref[h], preferred_element_type=F32)
        q = jnp.concatenate([qh[:, :MLA_NOPE], _rope_half(qh[:, MLA_NOPE:], cs)], axis=1) * scale
        q_ref[h] = q.T.astype(BF16)


def _kvprep_kernel(heads, q_rank, kv_rank, a_ref, g_ref, w_ref, cs_ref, k_ref, v_ref):
    ckv = _rms(a_ref[:, q_rank:q_rank + kv_rank], g_ref[...]).astype(BF16)
    kr = _rope_half(a_ref[:, q_rank + kv_rank:], cs_ref[...]).astype(BF16)
    row = lax.broadcasted_iota(jnp.int32, (MLA_V_PAD - MLA_V, a_ref.shape[0]), 0)
    ones_rows = jnp.where(row == 0, 1.0, 0.0)
    for h in range(heads):
        kvh = jnp.dot(ckv, w_ref[h], preferred_element_type=F32)
        k_ref[h, :, :MLA_NOPE] = kvh[:, :MLA_NOPE].astype(BF16)
        k_ref[h, :, MLA_NOPE:] = kr
        vt = kvh[:, MLA_NOPE:].T
        v_ref[h] = jnp.concatenate([vt, ones_rows], axis=0).astype(BF16)


def _qprep(a, row_off_blocks, n_rows, g_qn, wq, cs, heads, q_rank, scale, tm):
    aw = a.shape[1]
    return pl.pallas_call(
        functools.partial(_qprep_kernel, heads, q_rank, scale),
        grid=(n_rows // tm,),
        in_specs=[pl.BlockSpec((tm, aw), lambda i: (i + row_off_blocks, 0)),
                  pl.BlockSpec((1, q_rank), lambda i: (0, 0)),
                  pl.BlockSpec((heads, q_rank, MLA_QK_PAD), lambda i: (0, 0, 0)),
                  pl.BlockSpec((tm, 2 * MLA_ROPE), lambda i: (i + row_off_blocks, 0))],
        out_specs=pl.BlockSpec((heads, MLA_QK_PAD, tm), lambda i: (0, 0, i)),
        out_shape=jax.ShapeDtypeStruct((heads, MLA_QK_PAD, n_rows), BF16),
        compiler_params=_cparams(("parallel",)),
        name="mla_q_prep",
    )(a, g_qn, wq, cs)


def _kvprep(a, g_kvn, wkv, cs, heads, q_rank, kv_rank, tm):
    n_rows, aw = a.shape
    return pl.pallas_call(
        functools.partial(_kvprep_kernel, heads, q_rank, kv_rank),
        grid=(n_rows // tm,),
        in_specs=[pl.BlockSpec((tm, aw), lambda i: (i, 0)),
                  pl.BlockSpec((1, kv_rank), lambda i: (0, 0)),
                  pl.BlockSpec((heads, kv_rank, MLA_NOPE + MLA_V), lambda i: (0, 0, 0)),
                  pl.BlockSpec((tm, 2 * MLA_ROPE), lambda i: (i, 0))],
        out_specs=[pl.BlockSpec((heads, tm, MLA_QK_PAD), lambda i: (0, i, 0)),
                   pl.BlockSpec((heads, MLA_V_PAD, tm), lambda i: (0, 0, i))],
        out_shape=[jax.ShapeDtypeStruct((heads, n_rows, MLA_QK_PAD), BF16),
                   jax.ShapeDtypeStruct((heads, MLA_V_PAD, n_rows), BF16)],
        compiler_params=_cparams(("parallel",)),
        name="mla_kv_prep",
    )(a, g_kvn, wkv, cs)


def _dot_nt(a, b):
    return lax.dot_general(a, b, (((1,), (1,)), ((), ())), preferred_element_type=F32)


def _mla_kernel(tk, n_kv, n_sub, qt_ref, k_ref, vt_ref, o_ref, s_ref, p_ref, acc_ref):
    tq = qt_ref.shape[2] // n_sub
    n_chunks = n_kv // tk

    def trip(t, par, carry, do_a=True, do_b=True, do_c=True):
        ms, al_new, al_old = carry
        out_ms, out_al = ms, al_new
        if do_a:
            start = t * tk if isinstance(t, int) else pl.multiple_of(t * tk, tk)
            kc = k_ref[0, pl.ds(start, tk), :]
            out_ms, out_al = [], []
            for u in range(n_sub):
                s = jnp.dot(kc, qt_ref[0, :, u * tq:(u + 1) * tq], preferred_element_type=F32)
                s_ref[par, u] = s
                m_new = jnp.maximum(ms[u], jnp.max(s, axis=0, keepdims=True))
                out_al.append(jnp.exp2(ms[u] - m_new))
                out_ms.append(m_new)
            out_ms, out_al = tuple(out_ms), tuple(out_al)
        if do_b:
            for u in range(n_sub):
                p_ref[1 - par, u] = jnp.exp2((s_ref[1 - par, u] - ms[u]).astype(BF16))
        if do_c:
            start = (t - 2) * tk if isinstance(t, int) else pl.multiple_of((t - 2) * tk, tk)
            vc = vt_ref[0, :, pl.ds(start, tk)]
            for u in range(n_sub):
                acc_ref[u] = al_old[u] * acc_ref[u] + jnp.dot(vc, p_ref[par, u], preferred_element_type=F32)
        return out_ms, out_al, al_new

    acc_ref[...] = jnp.zeros_like(acc_ref)
    neg = tuple(jnp.full((1, tq), NEG_BIG, F32) for _ in range(n_sub))
    zero = tuple(jnp.zeros((1, tq), F32) for _ in range(n_sub))
    carry = (neg, zero, zero)
    carry = trip(0, 0, carry, do_b=False, do_c=False)
    if n_chunks > 1:
        carry = trip(1, 1, carry, do_c=False)
    n_steady = max(n_chunks - 2, 0)

    def double_trip(i, carry):
        t = 2 + 2 * i
        return trip(t + 1, 1, trip(t, 0, carry))

    carry = lax.fori_loop(0, n_steady // 2, double_trip, carry)
    if n_steady % 2 == 1:
        carry = trip(n_chunks - 1, (n_chunks - 1) % 2, carry)
    carry = trip(n_chunks, n_chunks % 2, carry, do_a=False, do_c=n_chunks > 1)
    if n_chunks > 1:
        carry = trip(n_chunks + 1, (n_chunks + 1) % 2, carry, do_a=False, do_b=False)
    else:
        carry = trip(2, 0, carry, do_a=False, do_b=False)
    for u in range(n_sub):
        o = acc_ref[u, :MLA_V, :] / acc_ref[u, MLA_V:MLA_V + 1, :]
        o_ref[u * tq:(u + 1) * tq, :] = o.T.astype(o_ref.dtype)


def _mla_attention(qt, k, vt):
    heads, _, n = qt.shape
    n_kv = k.shape[1]
    tq = _pick(n, (256, 128))
    n_sub = 4 if n % (4 * tq) == 0 else 1
    tk = _pick(n_kv, (1280, 768, 512, 256, 128))
    return pl.pallas_call(
        functools.partial(_mla_kernel, tk, n_kv, n_sub),
        grid=(heads, n // (tq * n_sub)),
        in_specs=[pl.BlockSpec((1, MLA_QK_PAD, tq * n_sub), lambda h, i: (h, 0, i)),
                  pl.BlockSpec((1, n_kv, MLA_QK_PAD), lambda h, i: (h, 0, 0)),
                  pl.BlockSpec((1, MLA_V_PAD, n_kv), lambda h, i: (h, 0, 0))],
        out_specs=pl.BlockSpec((tq * n_sub, MLA_V), lambda h, i: (i, h)),
        out_shape=jax.ShapeDtypeStruct((n, heads * MLA_V), BF16),
        scratch_shapes=[pltpu.VMEM((2, n_sub, tk, tq), F32),
                        pltpu.VMEM((2, n_sub, tk, tq), BF16),
                        pltpu.VMEM((n_sub, MLA_V_PAD, tq), F32)],
        compiler_params=_cparams(("parallel", "parallel")),
        name="mla_attention",
    )(qt, k, vt)


def _na_kernel(n_ctx, rows_total, win_h, key_rows, n_sub, q_ref, k_ref, vt_ref, t_ref, o_ref, bm_ref):
    b = pl.program_id(1)
    nb = pl.num_programs(1)
    r0 = b * NA_ROW_BLOCK
    base = jnp.clip(r0 - NA_WIN_H // 2, 0, rows_total - key_rows)

    @pl.when((b <= 1) | (b == nb - 1))
    def _():
        for i in range(NA_ROW_BLOCK):
            rs = jnp.clip(r0 + i - win_h // 2, 0, rows_total - win_h)
            for j in range(key_rows):
                kr = base + j
                valid = (kr >= rs) & (kr < rs + win_h)
                idx = jnp.clip(kr - (r0 + i) + NA_WIN_H - 1, 0, 2 * NA_WIN_H - 2)
                bm_ref[j * GRID_W:(j + 1) * GRID_W, i * GRID_W:(i + 1) * GRID_W] = jnp.where(
                    valid, t_ref[0, idx], NEG_BIG)

    n_loc = key_rows * GRID_W
    loc0 = pl.multiple_of(n_ctx + base * GRID_W, 128)
    k_ctx = k_ref[0:n_ctx, :]
    k_loc = k_ref[pl.ds(loc0, n_loc), :]
    vt_ctx = vt_ref[0, :, 0:n_ctx]
    vt_loc = vt_ref[0, :, pl.ds(loc0, n_loc)]
    tqs = q_ref.shape[0] // n_sub
    for u in range(n_sub):
        q = q_ref[u * tqs:(u + 1) * tqs, :]
        s_ctx = _dot_nt(k_ctx, q)
        s_loc = _dot_nt(k_loc, q) + bm_ref[:, u * tqs:(u + 1) * tqs]
        m = jnp.maximum(jnp.max(s_ctx, axis=0, keepdims=True), jnp.max(s_loc, axis=0, keepdims=True))
        p_ctx = jnp.exp2((s_ctx - m).astype(BF16))
        p_loc = jnp.exp2((s_loc - m).astype(BF16))
        acc = jnp.dot(vt_ctx, p_ctx, preferred_element_type=F32)
        acc = acc + jnp.dot(vt_loc, p_loc, preferred_element_type=F32)
        o = acc[:NA_DIM] / acc[NA_DIM:NA_DIM + 1]
        o_ref[u * tqs:(u + 1) * tqs, :] = o.T.astype(o_ref.dtype)


def _na_bias_table(rel_bias):
    cq = jnp.arange(GRID_W)
    ck = jnp.arange(GRID_W)
    col_start = jnp.clip(cq - NA_WIN_W // 2, 0, GRID_W - NA_WIN_W)
    valid = (ck[:, None] >= col_start[None, :]) & (ck[:, None] < col_start[None, :] + NA_WIN_W)
    co = jnp.clip(ck[:, None] - cq[None, :] + NA_WIN_W - 1, 0, 2 * NA_WIN_W - 2)
    onehot = (co[None] == jnp.arange(2 * NA_WIN_W - 1)[:, None, None]).astype(F32)
    t = jnp.einsum('hrc,ckq->hrkq', rel_bias, onehot, precision=lax.Precision.HIGHEST) * LOG2_E
    return jnp.where(valid[None, None], t, NEG_BIG)


def _vt_prep_kernel(heads, x_ref, o_ref):
    row = lax.broadcasted_iota(jnp.int32, (MLA_V_PAD - NA_DIM, x_ref.shape[0]), 0)
    ones_rows = jnp.where(row == 0, 1.0, 0.0)
    for h in range(heads):
        vt = x_ref[:, h * NA_DIM:(h + 1) * NA_DIM].astype(F32).T
        o_ref[h] = jnp.concatenate([vt, ones_rows], axis=0).astype(BF16)


def _na_values_transposed(kv, heads):
    n_kv = kv.shape[0]
    tm = _pick(n_kv, (256, 128))
    return pl.pallas_call(
        functools.partial(_vt_prep_kernel, heads),
        grid=(n_kv // tm,),
        in_specs=[pl.BlockSpec((tm, heads * NA_DIM), lambda i: (i, 1))],
        out_specs=pl.BlockSpec((heads, MLA_V_PAD, tm), lambda i: (0, 0, i)),
        out_shape=jax.ShapeDtypeStruct((heads, MLA_V_PAD, n_kv), BF16),
        compiler_params=_cparams(("parallel",)),
        name="na_v_prep",
    )(kv)


def _na_attention(q, kv, vt, table, heads, n_ctx):
    n = q.shape[0]
    n_kv = kv.shape[0]
    rows_total = n // GRID_W
    win_h = min(NA_WIN_H, rows_total)
    key_rows = min(NA_KEY_ROWS, rows_total)
    tq = NA_ROW_BLOCK * GRID_W
    n_sub = 2
    n_off = table.shape[1]
    return pl.pallas_call(
        functools.partial(_na_kernel, n_ctx, rows_total, win_h, key_rows, n_sub),
        grid=(heads, n // tq),
        in_specs=[pl.BlockSpec((tq, NA_DIM), lambda h, i: (i, h)),
                  pl.BlockSpec((n_kv, NA_DIM), lambda h, i: (0, h)),
                  pl.BlockSpec((1, MLA_V_PAD, n_kv), lambda h, i: (h, 0, 0)),
                  pl.BlockSpec((1, n_off, GRID_W, GRID_W), lambda h, i: (h, 0, 0, 0))],
        out_specs=pl.BlockSpec((tq, NA_DIM), lambda h, i: (i, h)),
        out_shape=jax.ShapeDtypeStruct((n, heads * NA_DIM), BF16),
        scratch_shapes=[pltpu.VMEM((key_rows * GRID_W, tq), F32)],
        compiler_params=_cparams(("parallel", "arbitrary")),
        name="na_attention",
    )(q, kv, vt, table)


def _pack_pair(hi, lo):
    h = lax.bitcast_convert_type(hi.astype(BF16).astype(F32), jnp.int32)
    l = lax.bitcast_convert_type(lo.astype(BF16).astype(F32), jnp.int32)
    return h | lax.shift_right_logical(l, 16)


def _unpack_pair(w):
    hi = lax.bitcast_convert_type(w & jnp.int32(-65536), F32).astype(BF16)
    lo = lax.bitcast_convert_type(lax.shift_left(w, 16), F32).astype(BF16)
    return hi, lo


def _fft_a_kernel(groups, x_ref, gate_ref, y_ref, g_ref, sh_ref, sc_ref, w_ref, xo_ref, z_ref):
    x = x_ref[...] + gate_ref[...] * y_ref[...]
    xo_ref[...] = x
    h = _modulated(x, g_ref[...], sh_ref[...], sc_ref[...]).astype(BF16)
    dg = h.shape[1] // groups
    for g in range(groups):
        z = jnp.dot(h[:, g * dg:(g + 1) * dg], w_ref[...], preferred_element_type=F32)
        z_ref[:, g * dg:(g + 1) * dg] = _pack_pair(z[:, :dg], z[:, dg:])


def _strided_fetch(src_hbm, buf, sem, step, slot, rows):
    return [pltpu.make_async_copy(src_hbm.at[:, step * rows + r, :], buf.at[slot, r], sem.at[slot, r])
            for r in range(rows)]


def _prefetch_next(src_hbm, buf, sem, rows):
    j = pl.program_id(0)
    slot = j % 2

    @pl.when(j == 0)
    def _():
        for c in _strided_fetch(src_hbm, buf, sem, 0, 0, rows):
            c.start()

    @pl.when(j + 1 < pl.num_programs(0))
    def _():
        for c in _strided_fetch(src_hbm, buf, sem, j + 1, 1 - slot, rows):
            c.start()

    return j, slot


def _fft_b_kernel(z_hbm, w_ref, y_ref, buf, sem):
    rows, n1 = y_ref.shape[0], y_ref.shape[1]
    j, slot = _prefetch_next(z_hbm, buf, sem, rows)
    waits = _strided_fetch(z_hbm, buf, sem, j, slot, rows)
    for r in range(rows):
        waits[r].wait()
        zr, zi = _unpack_pair(buf[slot, r])
        y = jnp.dot(w_ref[...], jnp.concatenate([zr, zi], axis=0), preferred_element_type=F32)
        y_ref[r] = _pack_pair(y[:n1], y[n1:])


def _fft_c_kernel(y_hbm, w_ref, f_hbm, buf, obuf, sem, osem):
    rows = buf.shape[1]
    half = obuf.shape[2]
    j, slot = _prefetch_next(y_hbm, buf, sem, rows)
    waits = _strided_fetch(y_hbm, buf, sem, j, slot, rows)
    puts = [pltpu.make_async_copy(obuf.at[r], f_hbm.at[:, j * rows + r, :], osem.at[r]) for r in range(rows)]
    for r in range(rows):
        waits[r].wait()
        yr, yi = _unpack_pair(buf[slot, r])
        f = jnp.dot(w_ref[r], jnp.concatenate([yr, yi], axis=0), preferred_element_type=F32)
        obuf[r] = _pack_pair(f[:, :half], f[:, half:])
        puts[r].start()
    for r in range(rows):
        puts[r].wait()


def _dft_tables(n, dg):
    lg = int(math.log2(n))
    n1 = 1 << ((lg + 1) // 2)
    n2 = n // n1
    scale_a = 1.0 / math.sqrt(dg)
    sb = 1 << (int(math.log2(n1)) // 2)
    scale_b = 1.0 / sb
    scale_c = 1.0 / (math.sqrt(n) / sb)
    ch = jnp.arange(dg, dtype=jnp.int32)
    ang = (2.0 * math.pi / dg) * ((ch[:, None] * ch[None, :]) % dg).astype(F32)
    wa = jnp.concatenate([jnp.cos(ang), -jnp.sin(ang)], axis=1) * scale_a
    a = jnp.arange(n1, dtype=jnp.int32)
    ang = (2.0 * math.pi / n1) * ((a[:, None] * a[None, :]) % n1).astype(F32)
    c, s = jnp.cos(ang), jnp.sin(ang)
    wb = jnp.concatenate([jnp.concatenate([c, s], axis=1), jnp.concatenate([-s, c], axis=1)], axis=0) * scale_b
    k1 = jnp.arange(n1, dtype=jnp.int32)[:, None, None]
    k2 = jnp.arange(n2, dtype=jnp.int32)[None, :, None]
    b = jnp.arange(n2, dtype=jnp.int32)[None, None, :]
    ang = (2.0 * math.pi / n) * ((b * (k1 + n1 * k2)) % n).astype(F32)
    wc = jnp.concatenate([jnp.cos(ang), jnp.sin(ang)], axis=2) * scale_c
    return n1, n2, wa.astype(BF16), wb.astype(BF16), wc.astype(BF16)


def _fourier_transform(x, gate, y, g, shift, scale):
    n, d = x.shape
    dg = d // FNET_GROUPS
    n1, n2, wa, wb, wc = _dft_tables(n, dg)
    tm = _pick(n, (512, 256, 128, 64, 8))
    row = pl.BlockSpec((tm, d), lambda i: (i, 0))
    vec = pl.BlockSpec((1, d), lambda i: (0, 0))
    x_new, z = pl.pallas_call(
        functools.partial(_fft_a_kernel, FNET_GROUPS),
        grid=(n // tm,),
        in_specs=[row, vec, row, vec, vec, vec, pl.BlockSpec((dg, 2 * dg), lambda i: (0, 0))],
        out_specs=[row, row],
        out_shape=[jax.ShapeDtypeStruct((n, d), F32),
                   jax.ShapeDtypeStruct((n, d), jnp.int32)],
        compiler_params=_cparams(("parallel",)),
        name="fft_channels",
    )(x, gate, y, g, shift, scale, wa)

    gb = _pick(n2, (FFT_ROWS_PER_STEP,))
    yw = pl.pallas_call(
        _fft_b_kernel,
        grid=(n2 // gb,),
        in_specs=[pl.BlockSpec(memory_space=pl.ANY), pl.BlockSpec((2 * n1, 2 * n1), lambda j: (0, 0))],
        out_specs=pl.BlockSpec((gb, n1, d), lambda j: (j, 0, 0)),
        out_shape=jax.ShapeDtypeStruct((n2, n1, d), jnp.int32),
        scratch_shapes=[pltpu.VMEM((2, gb, n1, d), jnp.int32), pltpu.SemaphoreType.DMA((2, gb))],
        compiler_params=_cparams(("arbitrary",)),
        name="fft_pos_outer",
    )(z.reshape(n1, n2, d), wb)

    gc = _pick(n1, (FFT_ROWS_PER_STEP,))
    f = pl.pallas_call(
        _fft_c_kernel,
        grid=(n1 // gc,),
        in_specs=[pl.BlockSpec(memory_space=pl.ANY), pl.BlockSpec((gc, n2, 2 * n2), lambda k: (k, 0, 0))],
        out_specs=pl.BlockSpec(memory_space=pl.ANY),
        out_shape=jax.ShapeDtypeStruct((n2, n1, d // 2), jnp.int32),
        scratch_shapes=[pltpu.VMEM((2, gc, n2, d), jnp.int32), pltpu.VMEM((gc, n2, d // 2), jnp.int32),
                        pltpu.SemaphoreType.DMA((2, gc)), pltpu.SemaphoreType.DMA((gc,))],
        compiler_params=_cparams(("arbitrary",)),
        name="fft_pos_inner",
    )(yw, wc)
    return x_new, f.reshape(n, d // 2)


def _expert_kernel(x_ref, wg_ref, wu_ref, wd_ref, gate_ref, o_ref, acc_ref):
    f = pl.program_id(2)

    @pl.when(f == 0)
    def _():
        acc_ref[...] = jnp.zeros_like(acc_ref)

    x = x_ref[0]
    a = jnp.dot(x, wg_ref[...].astype(BF16), preferred_element_type=F32)
    u = jnp.dot(x, wu_ref[...].astype(BF16), preferred_element_type=F32)
    hmid = (a / (1.0 + jnp.exp(-a)) * u).astype(BF16)
    acc_ref[...] += jnp.dot(hmid, wd_ref[...].astype(BF16), preferred_element_type=F32)

    @pl.when(f == pl.num_programs(2) - 1)
    def _():
        o_ref[0] = (acc_ref[...] * gate_ref[0]).astype(o_ref.dtype)


def _expert_ffn(xs, w_gate, w_up, w_down, gates, layer):
    e, cap, d = xs.shape
    ff = w_gate.shape[3]
    tm = _pick(cap, (1024, 512, 256, 128, 64, 8))
    tf = _pick(ff, (256, 128))
    return pl.pallas_call(
        _expert_kernel,
        grid=(e, cap // tm, ff // tf),
        in_specs=[pl.BlockSpec((1, tm, d), lambda ei, i, f: (ei, i, 0)),
                  pl.BlockSpec((None, None, d, tf), lambda ei, i, f: (layer, ei, 0, f)),
                  pl.BlockSpec((None, None, d, tf), lambda ei, i, f: (layer, ei, 0, f)),
                  pl.BlockSpec((None, None, tf, d), lambda ei, i, f: (layer, ei, f, 0)),
                  pl.BlockSpec((1, tm, 1), lambda ei, i, f: (ei, i, 0))],
        out_specs=pl.BlockSpec((1, tm, d), lambda ei, i, f: (ei, i, 0)),
        out_shape=jax.ShapeDtypeStruct((e, cap, d), BF16),
        scratch_shapes=[pltpu.VMEM((tm, d), F32)],
        compiler_params=_cparams(("parallel", "parallel", "arbitrary")),
        name="expert_ffn",
    )(xs, w_gate, w_up, w_down, gates)


def _combine_kernel(n_exp, cap, w0_ref, npass_ref, slot_ref, w0v_ref, y_hbm, r_ref, o_ref, buf_ref, sem):
    b = pl.program_id(0)
    nb = pl.num_programs(0)
    win = COMBINE_WIN
    par = b % 2
    slot = slot_ref[...]
    w0v = w0v_ref[...]
    jio = (lax.broadcasted_iota(jnp.int32, (1, n_exp * win), 1) % win).astype(F32)

    def window_copies(blk, p, buf_slot):
        copies = []
        for e in range(n_exp):
            st = pl.multiple_of(jnp.minimum(w0_ref[blk * n_exp + e] + p * win, cap - win), COMBINE_ALIGN)
            copies.append(pltpu.make_async_copy(y_hbm.at[e, pl.ds(st, win), :],
                                                buf_ref.at[buf_slot, pl.ds(e * win, win), :],
                                                sem.at[buf_slot, e]))
        return copies

    def onehot_of(p):
        lo = w0v + p * win
        st_vec = jnp.minimum(lo, cap - win)
        valid = (slot >= lo) & (slot < lo + win)
        rel = jnp.where(valid, slot - st_vec, -1).astype(F32).astype(BF16)
        relx = jnp.dot(rel, r_ref[...], preferred_element_type=F32)
        return jnp.where(relx == jio, 1.0, 0.0).astype(BF16)

    @pl.when(b == 0)
    def _():
        for c in window_copies(0, 0, 0):
            c.start()

    @pl.when(b + 1 < nb)
    def _():
        for c in window_copies(b + 1, 0, 1 - par):
            c.start()

    onehot = onehot_of(0)
    for c in window_copies(b, 0, par):
        c.wait()
    o_ref[...] = jnp.dot(onehot, buf_ref[par], preferred_element_type=F32)

    def extra_pass(p, carry):
        copies = window_copies(b, p, par)
        for c in copies:
            c.start()
        onehot = onehot_of(p)
        for c in copies:
            c.wait()
        o_ref[...] += jnp.dot(onehot, buf_ref[par], preferred_element_type=F32)
        return carry

    lax.fori_loop(1, npass_ref[b], extra_pass, 0)


def _combine(y, slot, w0, npass, n):
    n_exp, cap, d = y.shape
    tb = _pick(n, (COMBINE_BLOCK,))
    nb = n // tb
    win = COMBINE_WIN
    rows = lax.broadcasted_iota(jnp.int32, (ROUTER_LANES, n_exp * win), 0)
    cols = lax.broadcasted_iota(jnp.int32, (ROUTER_LANES, n_exp * win), 1)
    spread = (cols // win == rows).astype(BF16)
    w0v = jnp.zeros((nb, 1, ROUTER_LANES), jnp.int32).at[:, 0, :n_exp].set(w0)
    grid_spec = pltpu.PrefetchScalarGridSpec(
        num_scalar_prefetch=2,
        grid=(nb,),
        in_specs=[pl.BlockSpec((tb, ROUTER_LANES), lambda b, *_: (b, 0)),
                  pl.BlockSpec((None, 1, ROUTER_LANES), lambda b, *_: (b, 0, 0)),
                  pl.BlockSpec(memory_space=pl.ANY),
                  pl.BlockSpec((ROUTER_LANES, n_exp * win), lambda b, *_: (0, 0))],
        out_specs=pl.BlockSpec((tb, d), lambda b, *_: (b, 0)),
        scratch_shapes=[pltpu.VMEM((2, n_exp * win, d), BF16),
                        pltpu.SemaphoreType.DMA((2, n_exp))])
    return pl.pallas_call(
        functools.partial(_combine_kernel, n_exp, cap),
        grid_spec=grid_spec,
        out_shape=jax.ShapeDtypeStruct((n, d), F32),
        compiler_params=_cparams(("arbitrary",)),
        name="moe_combine",
    )(w0.reshape(-1), npass, slot, w0v, y, spread)


def _ec_moe(hf, aff, w_gate, w_up, w_down, layer):
    n, d = hf.shape
    e = w_gate.shape[1]
    cap = EC_CAPACITY * n // e
    aff_e = aff[:, :e]
    gates, idx = lax.top_k(aff_e.T, cap)
    idx_s, gates_s = lax.sort((idx, gates), dimension=1, num_keys=1)
    xs = jnp.take(hf, idx_s.reshape(-1), axis=0).reshape(e, cap, d)
    y = _expert_ffn(xs, w_gate, w_up, w_down, gates_s[..., None], layer)

    thr = gates[:, cap - 1][None, :]
    gt = aff_e > thr
    eq = aff_e == thr
    need = cap - jnp.sum(gt, axis=0, keepdims=True)
    eq_i = eq.astype(jnp.int32)
    sel = gt | (eq & (jnp.cumsum(eq_i, axis=0) - eq_i < need))
    sel_i = sel.astype(jnp.int32)
    pos = jnp.cumsum(sel_i, axis=0) - sel_i
    slot = jnp.full((n, ROUTER_LANES), -1, jnp.int32).at[:, :e].set(jnp.where(sel, pos, -1))
    tb = _pick(n, (COMBINE_BLOCK,))
    off = pos[::tb]
    cnt = jnp.sum(sel_i.reshape(n // tb, tb, e), axis=1)
    w0 = jnp.minimum(off // COMBINE_ALIGN * COMBINE_ALIGN, cap - COMBINE_WIN)
    npass = jnp.max(jnp.where(cnt > 0, (off + cnt - w0 + COMBINE_WIN - 1) // COMBINE_WIN, 0), axis=1)
    return _combine(y, slot, w0.astype(jnp.int32), npass.astype(jnp.int32), n)


def _rope_table(n_ctx, n):
    t = jnp.arange(n)
    n_freq = MLA_ROPE // 4
    inv_freq = ROPE_THETA ** (-jnp.arange(n_freq, dtype=F32) / n_freq)
    ar = (t // GRID_W).astype(F32)[:, None] * inv_freq
    ac = (t % GRID_W).astype(F32)[:, None] * inv_freq
    cos = jnp.concatenate([jnp.cos(ar), jnp.cos(ar), jnp.cos(ac), jnp.cos(ac)], axis=1)
    sin = jnp.concatenate([-jnp.sin(ar), jnp.sin(ar), -jnp.sin(ac), jnp.sin(ac)], axis=1)
    tab = jnp.concatenate([cos, sin], axis=1)
    ident = jnp.concatenate([jnp.ones((n_ctx, MLA_ROPE), F32), jnp.zeros((n_ctx, MLA_ROPE), F32)], axis=1)
    return jnp.concatenate([ident, tab], axis=0)


def _swap_pairs(w):
    q = MLA_ROPE // 4
    return jnp.concatenate([w[:, q:2 * q], w[:, :q], w[:, 3 * q:], w[:, 2 * q:3 * q]], axis=1)


def kernel(x, c, ctx, c_ctx, ada_w, ada_b, g_mix, g_ffn, g_final, w_in, g_qn, w_uq, g_kvn, w_ukv,
           na_rel_bias, w_out_attn, w_out_fourier, w_router, w_gate, w_up, w_down):
    _, n, d = x.shape
    n_ctx = ctx.shape[1]
    heads = d // 256
    q_rank = d // 4
    kv_rank = d // 8
    na_width = heads * NA_DIM
    mla_scale = (MLA_NOPE + MLA_ROPE) ** -0.5
    na_scale = NA_DIM ** -0.5
    xs = x[0]
    cx = ctx[0]

    cvecs = jnp.zeros((8, d), F32).at[0].set(c[0]).at[1].set(c_ctx)

    m0 = _ada_terms(cvecs, ada_w, ada_b, 0)
    sh_m, sc_m, g_m, sh_f, sc_f, g_f = [m0[0:1, k * d:(k + 1) * d] for k in range(6)]
    csh_m, csc_m = m0[1:2, 0:d], m0[1:2, d:2 * d]
    gm0 = g_mix[0][None]
    h_x = _modulate(xs, gm0, sh_m, sc_m)
    h_c = _modulate(cx, gm0, csh_m, csc_m)
    h_all = jnp.concatenate([h_c, h_x], axis=0)

    w0 = w_in[0]
    o1, o2, o3 = q_rank, q_rank + kv_rank, q_rank + kv_rank + MLA_ROPE
    w_kr = w0[:, o2:o3]
    w_a = jnp.concatenate([w0[:, :o2], w_kr, _swap_pairs(w_kr)], axis=1).astype(BF16)
    w_q = w0[:, o3:o3 + na_width].astype(BF16)
    w_kv = w0[:, o3 + na_width:].astype(BF16)
    ones = lambda k: jnp.ones((1, k), F32)
    a_all = _matmul(h_all, w_a, ones(w_a.shape[1]), F32)
    q_na = _matmul(h_x, w_q, jnp.full((1, na_width), na_scale * LOG2_E, F32), BF16)
    kv_na = _matmul(h_all, w_kv, ones(2 * na_width), BF16)

    cs = _rope_table(n_ctx, n)
    wq3 = w_uq[0].reshape(q_rank, heads, MLA_NOPE + MLA_ROPE)
    wq_rope = wq3[:, :, MLA_NOPE:]
    wq_swap = jnp.stack([_swap_pairs(wq_rope[:, h]) for h in range(heads)], axis=1)
    wq = jnp.concatenate([wq3, wq_swap], axis=2).transpose(1, 0, 2).astype(BF16)
    wkv = w_ukv[0].reshape(kv_rank, heads, MLA_NOPE + MLA_V).transpose(1, 0, 2).astype(BF16)
    tmp = _pick(math.gcd(n, n_ctx), (256, 128, 64, 8))
    qt_mla = _qprep(a_all, n_ctx // tmp, n, g_qn[0][None], wq, cs, heads, q_rank, mla_scale * LOG2_E, tmp)
    k_mla, vt_mla = _kvprep(a_all, g_kvn[0][None], wkv, cs, heads, q_rank, kv_rank, tmp)
    o_a = _mla_attention(qt_mla, k_mla, vt_mla)

    table = _na_bias_table(na_rel_bias[0])
    o_b = _na_attention(q_na, kv_na, _na_values_transposed(kv_na, heads), table, heads, n_ctx)
    x1 = _matmul_residual(o_a, o_b, w_out_attn[0].astype(BF16), xs, g_m)

    hf, aff = _modulate_router(x1, g_ffn[0][None], sh_f, sc_f, w_router[0])
    moe0 = _ec_moe(hf, aff, w_gate, w_up, w_down, 0)

    m1 = _ada_terms(cvecs, ada_w, ada_b, 1)
    sh_m1, sc_m1, g_m1, sh_f1, sc_f1, g_f1 = [m1[0:1, k * d:(k + 1) * d] for k in range(6)]
    x2, fr = _fourier_transform(x1, g_f, moe0, g_mix[1][None], sh_m1, sc_m1)
    x3 = _matmul_residual(fr, None, w_out_fourier[0].astype(BF16), x2, g_m1)
    hf1, aff1 = _modulate_router(x3, g_ffn[1][None], sh_f1, sc_f1, w_router[1])
    moe1 = _ec_moe(hf1, aff1, w_gate, w_up, w_down, 1)
    out = _final_norm(x3, g_f1, moe1, g_final[None])
    return out[None]
```

```python
import functools
import math

import jax
import jax.numpy as jnp
from jax import lax
from jax.experimental import pallas as pl
from jax.experimental.pallas import tpu as pltpu

F32 = jnp.float32
BF16 = jnp.bfloat16

GRID_W = 64
NORM_EPS = 1e-6
MLA_NOPE = 128
MLA_ROPE = 64
MLA_V = 128
MLA_QK_PAD = 256
MLA_V_PAD = 144
ROPE_THETA = 10000.0
NA_DIM = 128
NA_WIN_H = 8
NA_WIN_W = 16
NA_ROW_BLOCK = 8
NA_KEY_ROWS = 16
FNET_GROUPS = 8
FFT_ROWS_PER_STEP = 4
N_EXPERTS = 16
EC_CAPACITY = 2
ROUTER_LANES = 128
COMBINE_BLOCK = 256
COMBINE_WIN = 64
COMBINE_ALIGN = 16
NEG_BIG = -1e30
LOG2_E = math.log2(math.e)

V7X_VMEM_LIMIT_BYTES = 56 * 1024 * 1024


def _cparams(sem, flags=None):
    return pltpu.CompilerParams(dimension_semantics=sem, vmem_limit_bytes=V7X_VMEM_LIMIT_BYTES, flags=flags)


def _pick(n, candidates):
    for c in candidates:
        if n % c == 0:
            return c
    return n


def _ada_kernel(c_ref, w_ref, b_ref, o_ref):
    c = c_ref[...]
    s = c / (1.0 + jnp.exp(-c))
    o_ref[...] = jnp.dot(s, w_ref[...], preferred_element_type=F32,
                         precision=lax.Precision.HIGHEST) + b_ref[...]


def _ada_terms(cvecs, w, b, layer):
    depth, d, n6 = w.shape
    tn = _pick(n6, (1024, 512, 256, 128))
    return pl.pallas_call(
        _ada_kernel,
        grid=(n6 // tn,),
        in_specs=[pl.BlockSpec((8, d), lambda j: (0, 0)),
                  pl.BlockSpec((None, d, tn), lambda j: (layer, 0, j)),
                  pl.BlockSpec((None, 1, tn), lambda j: (layer, 0, j))],
        out_specs=pl.BlockSpec((8, tn), lambda j: (0, j)),
        out_shape=jax.ShapeDtypeStruct((8, n6), F32),
        compiler_params=_cparams(("parallel",)),
        name="ada_terms",
    )(cvecs, w, b.reshape(depth, 1, n6))


def _modulated(x, g, shift, scale):
    ms = jnp.mean(x * x, axis=-1, keepdims=True)
    y = x * lax.rsqrt(ms + NORM_EPS) * g
    return y * (1.0 + scale) + shift


def _mod_kernel(x_ref, g_ref, sh_ref, sc_ref, h_ref):
    h_ref[...] = _modulated(x_ref[...], g_ref[...], sh_ref[...], sc_ref[...]).astype(h_ref.dtype)


def _modulate(x, g, shift, scale, out_dtype=BF16):
    n, d = x.shape
    tm = _pick(n, (512, 256, 128, 64, 8))
    row = pl.BlockSpec((tm, d), lambda i: (i, 0))
    vec = pl.BlockSpec((1, d), lambda i: (0, 0))
    return pl.pallas_call(
        _mod_kernel,
        grid=(n // tm,),
        in_specs=[row, vec, vec, vec],
        out_specs=row,
        out_shape=jax.ShapeDtypeStruct((n, d), out_dtype),
        compiler_params=_cparams(("parallel",)),
        name="modulate",
    )(x, g, shift, scale)


def _router_kernel(n_experts, x_ref, g_ref, sh_ref, sc_ref, whi_ref, wlo_ref, h_ref, aff_ref):
    h = _modulated(x_ref[...], g_ref[...], sh_ref[...], sc_ref[...])
    h_hi = h.astype(BF16)
    h_ref[...] = h_hi
    h_lo = (h - h_hi.astype(F32)).astype(BF16)
    logits = (jnp.dot(h_hi, whi_ref[...], preferred_element_type=F32)
              + jnp.dot(h_hi, wlo_ref[...], preferred_element_type=F32)
              + jnp.dot(h_lo, whi_ref[...], preferred_element_type=F32))
    lane = lax.broadcasted_iota(jnp.int32, logits.shape, 1)
    logits = jnp.where(lane < n_experts, logits, NEG_BIG)
    m = jnp.max(logits, axis=-1, keepdims=True)
    p = jnp.exp(logits - m)
    aff_ref[...] = p / jnp.sum(p, axis=-1, keepdims=True)


def _modulate_router(x, g, shift, scale, w_router):
    n, d = x.shape
    n_experts = w_router.shape[1]
    wr = jnp.zeros((d, ROUTER_LANES), F32).at[:, :n_experts].set(w_router)
    wr_hi = wr.astype(BF16)
    wr_lo = (wr - wr_hi.astype(F32)).astype(BF16)
    tm = _pick(n, (512, 256, 128, 64, 8))
    row = pl.BlockSpec((tm, d), lambda i: (i, 0))
    vec = pl.BlockSpec((1, d), lambda i: (0, 0))
    wspec = pl.BlockSpec((d, ROUTER_LANES), lambda i: (0, 0))
    return pl.pallas_call(
        functools.partial(_router_kernel, n_experts),
        grid=(n // tm,),
        in_specs=[row, vec, vec, vec, wspec, wspec],
        out_specs=[row, pl.BlockSpec((tm, ROUTER_LANES), lambda i: (i, 0))],
        out_shape=[jax.ShapeDtypeStruct((n, d), BF16),
                   jax.ShapeDtypeStruct((n, ROUTER_LANES), F32)],
        compiler_params=_cparams(("parallel",)),
        name="modulate_router",
    )(x, g, shift, scale, wr_hi, wr_lo)


def _resid_kernel(x_ref, gate_ref, y_ref, o_ref):
    o_ref[...] = x_ref[...] + gate_ref[...] * y_ref[...]


def _residual(x, gate, y):
    n, d = x.shape
    tm = _pick(n, (512, 256, 128, 64, 8))
    row = pl.BlockSpec((tm, d), lambda i: (i, 0))
    vec = pl.BlockSpec((1, d), lambda i: (0, 0))
    return pl.pallas_call(
        _resid_kernel,
        grid=(n // tm,),
        in_specs=[row, vec, row],
        out_specs=row,
        out_shape=jax.ShapeDtypeStruct((n, d), F32),
        compiler_params=_cparams(("parallel",)),
        name="residual",
    )(x, gate, y)


def _final_kernel(x_ref, gate_ref, y_ref, g_ref, o_ref):
    x = x_ref[...] + gate_ref[...] * y_ref[...]
    ms = jnp.mean(x * x, axis=-1, keepdims=True)
    o_ref[...] = x * lax.rsqrt(ms + NORM_EPS) * g_ref[...]


def _final_norm(x, gate, y, g):
    n, d = x.shape
    tm = _pick(n, (512, 256, 128, 64, 8))
    row = pl.BlockSpec((tm, d), lambda i: (i, 0))
    vec = pl.BlockSpec((1, d), lambda i: (0, 0))
    return pl.pallas_call(
        _final_kernel,
        grid=(n // tm,),
        in_specs=[row, vec, row, vec],
        out_specs=row,
        out_shape=jax.ShapeDtypeStruct((n, d), F32),
        compiler_params=_cparams(("parallel",)),
        name="final_norm",
    )(x, gate, y, g)


def _mm_kernel(a_ref, w_ref, cs_ref, o_ref):
    acc = jnp.dot(a_ref[...], w_ref[...], preferred_element_type=F32)
    o_ref[...] = (acc * cs_ref[...]).astype(o_ref.dtype)


def _mm_resid_kernel(a1_ref, a2_ref, w_ref, x_ref, gate_ref, o_ref):
    k1 = a1_ref.shape[1]
    acc = jnp.dot(a1_ref[...], w_ref[:k1, :], preferred_element_type=F32)
    acc = acc + jnp.dot(a2_ref[...], w_ref[k1:, :], preferred_element_type=F32)
    o_ref[...] = x_ref[...] + gate_ref[...] * acc


def _mm_resid_f32_kernel(a_ref, w_ref, x_ref, gate_ref, o_ref):
    acc = jnp.dot(a_ref[...].astype(BF16), w_ref[...], preferred_element_type=F32)
    o_ref[...] = x_ref[...] + gate_ref[...] * acc


def _mm_tiles(m, n):
    tm = _pick(m, (1280, 1024, 512, 256, 128, 64, 8))
    tn = n if n <= 1024 else _pick(n, (1024, 512, 256, 128))
    return tm, tn


def _matmul(a, w, col_scale, out_dtype):
    m, k = a.shape
    n = w.shape[1]
    tm, tn = _mm_tiles(m, n)
    return pl.pallas_call(
        _mm_kernel,
        grid=(m // tm, n // tn),
        in_specs=[pl.BlockSpec((tm, k), lambda i, j: (i, 0)),
                  pl.BlockSpec((k, tn), lambda i, j: (0, j)),
                  pl.BlockSpec((1, tn), lambda i, j: (0, j))],
        out_specs=pl.BlockSpec((tm, tn), lambda i, j: (i, j)),
        out_shape=jax.ShapeDtypeStruct((m, n), out_dtype),
        compiler_params=_cparams(("parallel", "parallel")),
        name="matmul",
    )(a, w, col_scale)


def _matmul_residual(a1, a2, w, x, gate):
    m = a1.shape[0]
    n = w.shape[1]
    k1 = w.shape[0] // 2
    tm, tn = _mm_tiles(m, n)
    a_spec = pl.BlockSpec((tm, k1), lambda i, j: (i, 0))
    operands, a_specs, body = (a1, a2), [a_spec, a_spec], _mm_resid_kernel
    if a2 is None:
        operands, a_specs, body = (a1,), [pl.BlockSpec((tm, 2 * k1), lambda i, j: (i, 0))], _mm_resid_f32_kernel
    return pl.pallas_call(
        body,
        grid=(m // tm, n // tn),
        in_specs=a_specs + [pl.BlockSpec((2 * k1, tn), lambda i, j: (0, j)),
                            pl.BlockSpec((tm, tn), lambda i, j: (i, j)),
                            pl.BlockSpec((1, tn), lambda i, j: (0, j))],
        out_specs=pl.BlockSpec((tm, tn), lambda i, j: (i, j)),
        out_shape=jax.ShapeDtypeStruct((m, n), F32),
        compiler_params=_cparams(("parallel", "parallel")),
        name="matmul_residual",
    )(*operands, w, x, gate)


def _rms(x, g):
    ms = jnp.mean(x * x, axis=-1, keepdims=True)
    return x * lax.rsqrt(ms + NORM_EPS) * g


def _rope_half(pair, cs):
    t = pair * cs
    t = t + pltpu.roll(t, MLA_ROPE, 1)
    lane = lax.broadcasted_iota(jnp.int32, t.shape, 1)
    return jnp.where(lane < MLA_ROPE, t, 0.0)


def _qprep_kernel(heads, q_rank, scale, a_ref, g_ref, w_ref, cs_ref, q_ref):
    cq = _rms(a_ref[:, :q_rank], g_ref[...]).astype(BF16)
    cs = cs_ref[...]
    for h in range(heads):
        qh = jnp.dot(cq, w_ref[h], preferred_element_type=F32)
        q = jnp.concatenate([qh[:, :MLA_NOPE], _rope_half(qh[:, MLA_NOPE:], cs)], axis=1) * scale
        q_ref[h] = q.T.astype(BF16)


def _kvprep_kernel(heads, q_rank, kv_rank, a_ref, g_ref, w_ref, cs_ref, k_ref, v_ref):
    ckv = _rms(a_ref[:, q_rank:q_rank + kv_rank], g_ref[...]).astype(BF16)
    kr = _rope_half(a_ref[:, q_rank + kv_rank:], cs_ref[...]).astype(BF16)
    row = lax.broadcasted_iota(jnp.int32, (MLA_V_PAD - MLA_V, a_ref.shape[0]), 0)
    ones_rows = jnp.where(row == 0, 1.0, 0.0)
    for h in range(heads):
        kvh = jnp.dot(ckv, w_ref[h], preferred_element_type=F32)
        k_ref[h, :, :MLA_NOPE] = kvh[:, :MLA_NOPE].astype(BF16)
        k_ref[h, :, MLA_NOPE:] = kr
        vt = kvh[:, MLA_NOPE:].T
        v_ref[h] = jnp.concatenate([vt, ones_rows], axis=0).astype(BF16)


def _qprep(a, row_off_blocks, n_rows, g_qn, wq, cs, heads, q_rank, scale, tm):
    aw = a.shape[1]
    return pl.pallas_call(
        functools.partial(_qprep_kernel, heads, q_rank, scale),
        grid=(n_rows // tm,),
        in_specs=[pl.BlockSpec((tm, aw), lambda i: (i + row_off_blocks, 0)),
                  pl.BlockSpec((1, q_rank), lambda i: (0, 0)),
                  pl.BlockSpec((heads, q_rank, MLA_QK_PAD), lambda i: (0, 0, 0)),
                  pl.BlockSpec((tm, 2 * MLA_ROPE), lambda i: (i + row_off_blocks, 0))],
        out_specs=pl.BlockSpec((heads, MLA_QK_PAD, tm), lambda i: (0, 0, i)),
        out_shape=jax.ShapeDtypeStruct((heads, MLA_QK_PAD, n_rows), BF16),
        compiler_params=_cparams(("parallel",)),
        name="mla_q_prep",
    )(a, g_qn, wq, cs)


def _kvprep(a, g_kvn, wkv, cs, heads, q_rank, kv_rank, tm):
    n_rows, aw = a.shape
    return pl.pallas_call(
        functools.partial(_kvprep_kernel, heads, q_rank, kv_rank),
        grid=(n_rows // tm,),
        in_specs=[pl.BlockSpec((tm, aw), lambda i: (i, 0)),
                  pl.BlockSpec((1, kv_rank), lambda i: (0, 0)),
                  pl.BlockSpec((heads, kv_rank, MLA_NOPE + MLA_V), lambda i: (0, 0, 0)),
                  pl.BlockSpec((tm, 2 * MLA_ROPE), lambda i: (i, 0))],
        out_specs=[pl.BlockSpec((heads, tm, MLA_QK_PAD), lambda i: (0, i, 0)),
                   pl.BlockSpec((heads, MLA_V_PAD, tm), lambda i: (0, 0, i))],
        out_shape=[jax.ShapeDtypeStruct((heads, n_rows, MLA_QK_PAD), BF16),
                   jax.ShapeDtypeStruct((heads, MLA_V_PAD, n_rows), BF16)],
        compiler_params=_cparams(("parallel",)),
        name="mla_kv_prep",
    )(a, g_kvn, wkv, cs)


def _dot_nt(a, b):
    return lax.dot_general(a, b, (((1,), (1,)), ((), ())), preferred_element_type=F32)


def _mla_kernel(tk, n_kv, n_sub, qt_ref, k_ref, vt_ref, o_ref, s_ref, p_ref, acc_ref):
    tq = qt_ref.shape[2] // n_sub
    n_chunks = n_kv // tk

    def trip(t, par, carry, do_a=True, do_b=True, do_c=True):
        ms, al_new, al_old = carry
        out_ms, out_al = ms, al_new
        if do_a:
            start = t * tk if isinstance(t, int) else pl.multiple_of(t * tk, tk)
            kc = k_ref[0, pl.ds(start, tk), :]
            out_ms, out_al = [], []
            for u in range(n_sub):
                s = jnp.dot(kc, qt_ref[0, :, u * tq:(u + 1) * tq], preferred_element_type=F32)
                s_ref[par, u] = s
                m_new = jnp.maximum(ms[u], jnp.max(s, axis=0, keepdims=True))
                out_al.append(jnp.exp2(ms[u] - m_new))
                out_ms.append(m_new)
            out_ms, out_al = tuple(out_ms), tuple(out_al)
        if do_b:
            for u in range(n_sub):
                p_ref[1 - par, u] = jnp.exp2((s_ref[1 - par, u] - ms[u]).astype(BF16))
        if do_c:
            start = (t - 2) * tk if isinstance(t, int) else pl.multiple_of((t - 2) * tk, tk)
            vc = vt_ref[0, :, pl.ds(start, tk)]
            for u in range(n_sub):
                acc_ref[u] = al_old[u] * acc_ref[u] + jnp.dot(vc, p_ref[par, u], preferred_element_type=F32)
        return out_ms, out_al, al_new

    acc_ref[...] = jnp.zeros_like(acc_ref)
    neg = tuple(jnp.full((1, tq), NEG_BIG, F32) for _ in range(n_sub))
    zero = tuple(jnp.zeros((1, tq), F32) for _ in range(n_sub))
    carry = (neg, zero, zero)
    carry = trip(0, 0, carry, do_b=False, do_c=False)
    if n_chunks > 1:
        carry = trip(1, 1, carry, do_c=False)
    n_steady = max(n_chunks - 2, 0)

    def double_trip(i, carry):
        t = 2 + 2 * i
        return trip(t + 1, 1, trip(t, 0, carry))

    carry = lax.fori_loop(0, n_steady // 2, double_trip, carry)
    if n_steady % 2 == 1:
        carry = trip(n_chunks - 1, (n_chunks - 1) % 2, carry)
    carry = trip(n_chunks, n_chunks % 2, carry, do_a=False, do_c=n_chunks > 1)
    if n_chunks > 1:
        carry = trip(n_chunks + 1, (n_chunks + 1) % 2, carry, do_a=False, do_b=False)
    else:
        carry = trip(2, 0, carry, do_a=False, do_b=False)
    for u in range(n_sub):
        o = acc_ref[u, :MLA_V, :] / acc_ref[u, MLA_V:MLA_V + 1, :]
        o_ref[u * tq:(u + 1) * tq, :] = o.T.astype(o_ref.dtype)


def _mla_attention(qt, k, vt):
    heads, _, n = qt.shape
    n_kv = k.shape[1]
    tq = _pick(n, (256, 128))
    n_sub = 4 if n % (4 * tq) == 0 else 1
    tk = _pick(n_kv, (1280, 768, 512, 256, 128))
    return pl.pallas_call(
        functools.partial(_mla_kernel, tk, n_kv, n_sub),
        grid=(heads, n // (tq * n_sub)),
        in_specs=[pl.BlockSpec((1, MLA_QK_PAD, tq * n_sub), lambda h, i: (h, 0, i)),
                  pl.BlockSpec((1, n_kv, MLA_QK_PAD), lambda h, i: (h, 0, 0)),
                  pl.BlockSpec((1, MLA_V_PAD, n_kv), lambda h, i: (h, 0, 0))],
        out_specs=pl.BlockSpec((tq * n_sub, MLA_V), lambda h, i: (i, h)),
        out_shape=jax.ShapeDtypeStruct((n, heads * MLA_V), BF16),
        scratch_shapes=[pltpu.VMEM((2, n_sub, tk, tq), F32),
                        pltpu.VMEM((2, n_sub, tk, tq), BF16),
                        pltpu.VMEM((n_sub, MLA_V_PAD, tq), F32)],
        compiler_params=_cparams(("parallel", "parallel")),
        name="mla_attention",
    )(qt, k, vt)


def _na_kernel(n_ctx, rows_total, win_h, key_rows, q_ref, k_ref, vt_ref, t_ref, o_ref,
               bm_ref, s_ref, p_ref, m_ref):
    hd = pl.program_id(0)
    b = pl.program_id(1)
    nb = rows_total // NA_ROW_BLOCK
    n_loc = key_rows * GRID_W
    par = b % 2

    def window_start(blk):
        base = jnp.clip(blk * NA_ROW_BLOCK - NA_WIN_H // 2, 0, rows_total - key_rows)
        return pl.multiple_of(n_ctx + base * GRID_W, 128)

    @pl.when((hd == 0) & (b == 0))
    def _():
        s_ref[...] = jnp.zeros_like(s_ref)
        p_ref[...] = jnp.ones_like(p_ref)
        m_ref[...] = jnp.zeros_like(m_ref)

    @pl.when(b == 0)
    def _():
        for v, blk in enumerate((0, min(1, nb - 1), nb - 1)):
            r0 = blk * NA_ROW_BLOCK
            base = min(max(r0 - NA_WIN_H // 2, 0), rows_total - key_rows)
            for i in range(NA_ROW_BLOCK):
                rs = min(max(r0 + i - win_h // 2, 0), rows_total - win_h)
                for j in range(key_rows):
                    kr = base + j
                    tile = (slice(j * GRID_W, (j + 1) * GRID_W), slice(i * GRID_W, (i + 1) * GRID_W))
                    if rs <= kr < rs + win_h:
                        bm_ref[(v,) + tile] = t_ref[0, kr - (r0 + i) + NA_WIN_H - 1]
                    else:
                        bm_ref[(v,) + tile] = jnp.full((GRID_W, GRID_W), NEG_BIG, F32)

    def stages(par):
        ta = jnp.minimum(b, nb - 1)
        variant = jnp.where(ta == 0, 0, jnp.where(ta == nb - 1, 2, 1))
        q = q_ref[...]
        s_ctx = _dot_nt(k_ref[0:n_ctx, :], q)
        s_loc = _dot_nt(k_ref[pl.ds(window_start(ta), n_loc), :], q) + bm_ref[variant]
        s_ref[par, 0:n_ctx, :] = s_ctx
        s_ref[par, n_ctx:, :] = s_loc
        m_ref[par] = jnp.maximum(jnp.max(s_ctx, axis=0, keepdims=True), jnp.max(s_loc, axis=0, keepdims=True))

        p_ref[1 - par] = jnp.exp2((s_ref[1 - par] - m_ref[1 - par]).astype(BF16))

        tc = jnp.clip(b - 2, 0, nb - 1)
        acc = jnp.dot(vt_ref[0, :, 0:n_ctx], p_ref[par, 0:n_ctx, :], preferred_element_type=F32)
        acc = acc + jnp.dot(vt_ref[0, :, pl.ds(window_start(tc), n_loc)], p_ref[par, n_ctx:, :],
                            preferred_element_type=F32)
        o = acc[:NA_DIM] / acc[NA_DIM:NA_DIM + 1]
        o_ref[...] = o.T.astype(o_ref.dtype)

    for static_par in (0, 1):
        pl.when(par == static_par)(functools.partial(stages, static_par))


def _na_bias_table(rel_bias):
    cq = jnp.arange(GRID_W)
    ck = jnp.arange(GRID_W)
    col_start = jnp.clip(cq - NA_WIN_W // 2, 0, GRID_W - NA_WIN_W)
    valid = (ck[:, None] >= col_start[None, :]) & (ck[:, None] < col_start[None, :] + NA_WIN_W)
    co = jnp.clip(ck[:, None] - cq[None, :] + NA_WIN_W - 1, 0, 2 * NA_WIN_W - 2)
    onehot = (co[None] == jnp.arange(2 * NA_WIN_W - 1)[:, None, None]).astype(F32)
    t = jnp.einsum('hrc,ckq->hrkq', rel_bias, onehot, precision=lax.Precision.HIGHEST) * LOG2_E
    return jnp.where(valid[None, None], t, NEG_BIG)


def _vt_prep_kernel(heads, x_ref, o_ref):
    row = lax.broadcasted_iota(jnp.int32, (MLA_V_PAD - NA_DIM, x_ref.shape[0]), 0)
    ones_rows = jnp.where(row == 0, 1.0, 0.0)
    for h in range(heads):
        vt = x_ref[:, h * NA_DIM:(h + 1) * NA_DIM].astype(F32).T
        o_ref[h] = jnp.concatenate([vt, ones_rows], axis=0).astype(BF16)


def _na_values_transposed(kv, heads):
    n_kv = kv.shape[0]
    tm = _pick(n_kv, (256, 128))
    return pl.pallas_call(
        functools.partial(_vt_prep_kernel, heads),
        grid=(n_kv // tm,),
        in_specs=[pl.BlockSpec((tm, heads * NA_DIM), lambda i: (i, 1))],
        out_specs=pl.BlockSpec((heads, MLA_V_PAD, tm), lambda i: (0, 0, i)),
        out_shape=jax.ShapeDtypeStruct((heads, MLA_V_PAD, n_kv), BF16),
        compiler_params=_cparams(("parallel",)),
        name="na_v_prep",
    )(kv)


def _na_attention(q, kv, vt, table, heads, n_ctx):
    n = q.shape[0]
    n_kv = kv.shape[0]
    rows_total = n // GRID_W
    win_h = min(NA_WIN_H, rows_total)
    key_rows = min(NA_KEY_ROWS, rows_total)
    tq = NA_ROW_BLOCK * GRID_W
    nb = n // tq
    n_loc = key_rows * GRID_W
    n_off = table.shape[1]
    return pl.pallas_call(
        functools.partial(_na_kernel, n_ctx, rows_total, win_h, key_rows),
        grid=(heads, nb + 2),
        in_specs=[pl.BlockSpec((tq, NA_DIM), lambda h, i: (jnp.minimum(i, nb - 1), h)),
                  pl.BlockSpec((n_kv, NA_DIM), lambda h, i: (0, h)),
                  pl.BlockSpec((1, MLA_V_PAD, n_kv), lambda h, i: (h, 0, 0)),
                  pl.BlockSpec((1, n_off, GRID_W, GRID_W), lambda h, i: (h, 0, 0, 0))],
        out_specs=pl.BlockSpec((tq, NA_DIM), lambda h, i: (jnp.clip(i - 2, 0, nb - 1), h)),
        out_shape=jax.ShapeDtypeStruct((n, heads * NA_DIM), BF16),
        scratch_shapes=[pltpu.VMEM((3, n_loc, tq), F32),
                        pltpu.VMEM((2, n_ctx + n_loc, tq), F32),
                        pltpu.VMEM((2, n_ctx + n_loc, tq), BF16),
                        pltpu.VMEM((2, 1, tq), F32)],
        compiler_params=_cparams(("arbitrary", "arbitrary")),
        name="na_attention",
    )(q, kv, vt, table)


def _fft_a_kernel(groups, x_ref, gate_ref, y_ref, g_ref, sh_ref, sc_ref, w_ref, xo_ref, z_ref):
    x = x_ref[...] + gate_ref[...] * y_ref[...]
    xo_ref[...] = x
    h = _modulated(x, g_ref[...], sh_ref[...], sc_ref[...]).astype(BF16)
    d = h.shape[1]
    dg = d // groups
    for g in range(groups):
        z = jnp.dot(h[:, g * dg:(g + 1) * dg], w_ref[...], preferred_element_type=F32)
        z_ref[:, g * dg:(g + 1) * dg] = z[:, :dg]
        z_ref[:, d + g * dg:d + (g + 1) * dg] = z[:, dg:]


def _strided_fetch(src_hbm, buf, sem, step, slot, rows):
    return [pltpu.make_async_copy(src_hbm.at[:, step * rows + r, :], buf.at[slot, r], sem.at[slot, r])
            for r in range(rows)]


def _prefetch_next(src_hbm, buf, sem, rows):
    j = pl.program_id(0)
    slot = j % 2

    @pl.when(j == 0)
    def _():
        for c in _strided_fetch(src_hbm, buf, sem, 0, 0, rows):
            c.start()

    @pl.when(j + 1 < pl.num_programs(0))
    def _():
        for c in _strided_fetch(src_hbm, buf, sem, j + 1, 1 - slot, rows):
            c.start()

    return j, slot


def _fft_b_kernel(z_hbm, w_ref, y_ref, buf, sem):
    rows, n1 = y_ref.shape[0], y_ref.shape[1]
    d = y_ref.shape[2] // 2
    j, slot = _prefetch_next(z_hbm, buf, sem, rows)
    waits = _strided_fetch(z_hbm, buf, sem, j, slot, rows)
    for r in range(rows):
        waits[r].wait()
        z = jnp.concatenate([buf[slot, r, :, :d], buf[slot, r, :, d:]], axis=0).astype(BF16)
        y = jnp.dot(w_ref[...], z, preferred_element_type=F32)
        y_ref[r, :, :d] = y[:n1]
        y_ref[r, :, d:] = y[n1:]


def _fft_c_kernel(y_hbm, w_ref, f_hbm, buf, obuf, sem, osem):
    rows = buf.shape[1]
    d = obuf.shape[2]
    j, slot = _prefetch_next(y_hbm, buf, sem, rows)
    waits = _strided_fetch(y_hbm, buf, sem, j, slot, rows)
    puts = [pltpu.make_async_copy(obuf.at[r], f_hbm.at[:, j * rows + r, :], osem.at[r]) for r in range(rows)]
    for r in range(rows):
        waits[r].wait()
        y = jnp.concatenate([buf[slot, r, :, :d], buf[slot, r, :, d:]], axis=0).astype(BF16)
        obuf[r] = jnp.dot(w_ref[r], y, preferred_element_type=F32)
        puts[r].start()
    for r in range(rows):
        puts[r].wait()


def _dft_tables(n, dg):
    lg = int(math.log2(n))
    n1 = 1 << ((lg + 1) // 2)
    n2 = n // n1
    scale_a = 1.0 / math.sqrt(dg)
    sb = 1 << (int(math.log2(n1)) // 2)
    scale_b = 1.0 / sb
    scale_c = 1.0 / (math.sqrt(n) / sb)
    ch = jnp.arange(dg, dtype=jnp.int32)
    ang = (2.0 * math.pi / dg) * ((ch[:, None] * ch[None, :]) % dg).astype(F32)
    wa = jnp.concatenate([jnp.cos(ang), -jnp.sin(ang)], axis=1) * scale_a
    a = jnp.arange(n1, dtype=jnp.int32)
    ang = (2.0 * math.pi / n1) * ((a[:, None] * a[None, :]) % n1).astype(F32)
    c, s = jnp.cos(ang), jnp.sin(ang)
    wb = jnp.concatenate([jnp.concatenate([c, s], axis=1), jnp.concatenate([-s, c], axis=1)], axis=0) * scale_b
    k1 = jnp.arange(n1, dtype=jnp.int32)[:, None, None]
    k2 = jnp.arange(n2, dtype=jnp.int32)[None, :, None]
    b = jnp.arange(n2, dtype=jnp.int32)[None, None, :]
    ang = (2.0 * math.pi / n) * ((b * (k1 + n1 * k2)) % n).astype(F32)
    wc = jnp.concatenate([jnp.cos(ang), jnp.sin(ang)], axis=2) * scale_c
    return n1, n2, wa.astype(BF16), wb.astype(BF16), wc.astype(BF16)


def _fourier_transform(x, gate, y, g, shift, scale):
    n, d = x.shape
    dg = d // FNET_GROUPS
    n1, n2, wa, wb, wc = _dft_tables(n, dg)
    tm = _pick(n, (512, 256, 128, 64, 8))
    row = pl.BlockSpec((tm, d), lambda i: (i, 0))
    vec = pl.BlockSpec((1, d), lambda i: (0, 0))
    x_new, z = pl.pallas_call(
        functools.partial(_fft_a_kernel, FNET_GROUPS),
        grid=(n // tm,),
        in_specs=[row, vec, row, vec, vec, vec, pl.BlockSpec((dg, 2 * dg), lambda i: (0, 0))],
        out_specs=[row, pl.BlockSpec((tm, 2 * d), lambda i: (i, 0))],
        out_shape=[jax.ShapeDtypeStruct((n, d), F32),
                   jax.ShapeDtypeStruct((n, 2 * d), F32)],
        compiler_params=_cparams(("parallel",)),
        name="fft_channels",
    )(x, gate, y, g, shift, scale, wa)

    gb = _pick(n2, (FFT_ROWS_PER_STEP,))
    yw = pl.pallas_call(
        _fft_b_kernel,
        grid=(n2 // gb,),
        in_specs=[pl.BlockSpec(memory_space=pl.ANY), pl.BlockSpec((2 * n1, 2 * n1), lambda j: (0, 0))],
        out_specs=pl.BlockSpec((gb, n1, 2 * d), lambda j: (j, 0, 0)),
        out_shape=jax.ShapeDtypeStruct((n2, n1, 2 * d), F32),
        scratch_shapes=[pltpu.VMEM((2, gb, n1, 2 * d), F32), pltpu.SemaphoreType.DMA((2, gb))],
        compiler_params=_cparams(("arbitrary",)),
        name="fft_pos_outer",
    )(z.reshape(n1, n2, 2 * d), wb)

    gc = _pick(n1, (FFT_ROWS_PER_STEP,))
    f = pl.pallas_call(
        _fft_c_kernel,
        grid=(n1 // gc,),
        in_specs=[pl.BlockSpec(memory_space=pl.ANY), pl.BlockSpec((gc, n2, 2 * n2), lambda k: (k, 0, 0))],
        out_specs=pl.BlockSpec(memory_space=pl.ANY),
        out_shape=jax.ShapeDtypeStruct((n2, n1, d), F32),
        scratch_shapes=[pltpu.VMEM((2, gc, n2, 2 * d), F32), pltpu.VMEM((gc, n2, d), F32),
                        pltpu.SemaphoreType.DMA((2, gc)), pltpu.SemaphoreType.DMA((gc,))],
        compiler_params=_cparams(("arbitrary",)),
        name="fft_pos_inner",
    )(yw, wc)
    return x_new, f.reshape(n, d)


def _expert_kernel(x_ref, wg_ref, wu_ref, wd_ref, gate_ref, o_ref, acc_ref):
    f = pl.program_id(2)

    @pl.when(f == 0)
    def _():
        acc_ref[...] = jnp.zeros_like(acc_ref)

    x = x_ref[0]
    a = jnp.dot(x, wg_ref[...].astype(BF16), preferred_element_type=F32)
    u = jnp.dot(x, wu_ref[...].astype(BF16), preferred_element_type=F32)
    hmid = (a / (1.0 + jnp.exp(-a)) * u).astype(BF16)
    acc_ref[...] += jnp.dot(hmid, wd_ref[...].astype(BF16), preferred_element_type=F32)

    @pl.when(f == pl.num_programs(2) - 1)
    def _():
        o_ref[0] = (acc_ref[...] * gate_ref[0]).astype(o_ref.dtype)


def _expert_ffn(xs, w_gate, w_up, w_down, gates, layer):
    e, cap, d = xs.shape
    ff = w_gate.shape[3]
    tm = _pick(cap, (1024, 512, 256, 128, 64, 8))
    tf = _pick(ff, (256, 128))
    return pl.pallas_call(
        _expert_kernel,
        grid=(e, cap // tm, ff // tf),
        in_specs=[pl.BlockSpec((1, tm, d), lambda ei, i, f: (ei, i, 0)),
                  pl.BlockSpec((None, None, d, tf), lambda ei, i, f: (layer, ei, 0, f)),
                  pl.BlockSpec((None, None, d, tf), lambda ei, i, f: (layer, ei, 0, f)),
                  pl.BlockSpec((None, None, tf, d), lambda ei, i, f: (layer, ei, f, 0)),
                  pl.BlockSpec((1, tm, 1), lambda ei, i, f: (ei, i, 0))],
        out_specs=pl.BlockSpec((1, tm, d), lambda ei, i, f: (ei, i, 0)),
        out_shape=jax.ShapeDtypeStruct((e, cap, d), BF16),
        scratch_shapes=[pltpu.VMEM((tm, d), F32)],
        compiler_params=_cparams(("parallel", "parallel", "arbitrary")),
        name="expert_ffn",
    )(xs, w_gate, w_up, w_down, gates)


def _combine_kernel(n_exp, cap, w0_ref, npass_ref, slot_ref, w0v_ref, y_hbm, r_ref, o_ref, buf_ref, acc_ref, sem):
    b = pl.program_id(0)
    nb = pl.num_programs(0)
    win = COMBINE_WIN
    par = b % 2
    slot = slot_ref[...]
    w0v = w0v_ref[...]
    jio = (lax.broadcasted_iota(jnp.int32, (1, n_exp * win), 1) % win).astype(F32)

    def window_copies(blk, p, buf_slot):
        copies = []
        for e in range(n_exp):
            st = pl.multiple_of(jnp.minimum(w0_ref[blk * n_exp + e] + p * win, cap - win), COMBINE_ALIGN)
            copies.append(pltpu.make_async_copy(y_hbm.at[e, pl.ds(st, win), :],
                                                buf_ref.at[buf_slot, pl.ds(e * win, win), :],
                                                sem.at[buf_slot, e]))
        return copies

    def onehot_of(p):
        lo = w0v + p * win
        st_vec = jnp.minimum(lo, cap - win)
        valid = (slot >= lo) & (slot < lo + win)
        rel = jnp.where(valid, slot - st_vec, -1).astype(F32).astype(BF16)
        relx = jnp.dot(rel, r_ref[...], preferred_element_type=F32)
        return jnp.where(relx == jio, 1.0, 0.0).astype(BF16)

    @pl.when(b == 0)
    def _():
        for c in window_copies(0, 0, 0):
            c.start()

    @pl.when(b + 1 < nb)
    def _():
        for c in window_copies(b + 1, 0, 1 - par):
            c.start()

    onehot = onehot_of(0)
    for c in window_copies(b, 0, par):
        c.wait()
    acc_ref[...] = jnp.dot(onehot, buf_ref[par], preferred_element_type=F32)

    def extra_pass(p, carry):
        copies = window_copies(b, p, par)
        for c in copies:
            c.start()
        onehot = onehot_of(p)
        for c in copies:
            c.wait()
        acc_ref[...] += jnp.dot(onehot, buf_ref[par], preferred_element_type=F32)
        return carry

    lax.fori_loop(1, npass_ref[b], extra_pass, 0)
    o_ref[...] = acc_ref[...].astype(o_ref.dtype)


def _combine(y, slot, w0, npass, n):
    n_exp, cap, d = y.shape
    tb = _pick(n, (COMBINE_BLOCK,))
    nb = n // tb
    win = COMBINE_WIN
    rows = lax.broadcasted_iota(jnp.int32, (ROUTER_LANES, n_exp * win), 0)
    cols = lax.broadcasted_iota(jnp.int32, (ROUTER_LANES, n_exp * win), 1)
    spread = (cols // win == rows).astype(BF16)
    w0v = jnp.zeros((nb, 1, ROUTER_LANES), jnp.int32).at[:, 0, :n_exp].set(w0)
    grid_spec = pltpu.PrefetchScalarGridSpec(
        num_scalar_prefetch=2,
        grid=(nb,),
        in_specs=[pl.BlockSpec((tb, ROUTER_LANES), lambda b, *_: (b, 0)),
                  pl.BlockSpec((None, 1, ROUTER_LANES), lambda b, *_: (b, 0, 0)),
                  pl.BlockSpec(memory_space=pl.ANY),
                  pl.BlockSpec((ROUTER_LANES, n_exp * win), lambda b, *_: (0, 0))],
        out_specs=pl.BlockSpec((tb, d), lambda b, *_: (b, 0)),
        scratch_shapes=[pltpu.VMEM((2, n_exp * win, d), BF16),
                        pltpu.VMEM((tb, d), F32),
                        pltpu.SemaphoreType.DMA((2, n_exp))])
    return pl.pallas_call(
        functools.partial(_combine_kernel, n_exp, cap),
        grid_spec=grid_spec,
        out_shape=jax.ShapeDtypeStruct((n, d), BF16),
        compiler_params=_cparams(("arbitrary",)),
        name="moe_combine",
    )(w0.reshape(-1), npass, slot, w0v, y, spread)


def _ec_moe(hf, aff, w_gate, w_up, w_down, layer):
    n, d = hf.shape
    e = w_gate.shape[1]
    cap = EC_CAPACITY * n // e
    aff_e = aff[:, :e]
    gates, idx = lax.top_k(aff_e.T, cap)
    idx_s, gates_s = lax.sort((idx, gates), dimension=1, num_keys=1)
    xs = jnp.take(hf, idx_s.reshape(-1), axis=0).reshape(e, cap, d)
    y = _expert_ffn(xs, w_gate, w_up, w_down, gates_s[..., None], layer)

    thr = gates[:, cap - 1][None, :]
    gt = aff_e > thr
    eq = aff_e == thr
    need = cap - jnp.sum(gt, axis=0, keepdims=True)
    eq_i = eq.astype(jnp.int32)
    sel = gt | (eq & (jnp.cumsum(eq_i, axis=0) - eq_i < need))
    sel_i = sel.astype(jnp.int32)
    pos = jnp.cumsum(sel_i, axis=0) - sel_i
    slot = jnp.full((n, ROUTER_LANES), -1, jnp.int32).at[:, :e].set(jnp.where(sel, pos, -1))
    tb = _pick(n, (COMBINE_BLOCK,))
    off = pos[::tb]
    cnt = jnp.sum(sel_i.reshape(n // tb, tb, e), axis=1)
    w0 = jnp.minimum(off // COMBINE_ALIGN * COMBINE_ALIGN, cap - COMBINE_WIN)
    npass = jnp.max(jnp.where(cnt > 0, (off + cnt - w0 + COMBINE_WIN - 1) // COMBINE_WIN, 0), axis=1)
    return _combine(y, slot, w0.astype(jnp.int32), npass.astype(jnp.int32), n)


def _rope_table(n_ctx, n):
    t = jnp.arange(n)
    n_freq = MLA_ROPE // 4
    inv_freq = ROPE_THETA ** (-jnp.arange(n_freq, dtype=F32) / n_freq)
    ar = (t // GRID_W).astype(F32)[:, None] * inv_freq
    ac = (t % GRID_W).astype(F32)[:, None] * inv_freq
    cos = jnp.concatenate([jnp.cos(ar), jnp.cos(ar), jnp.cos(ac), jnp.cos(ac)], axis=1)
    sin = jnp.concatenate([-jnp.sin(ar), jnp.sin(ar), -jnp.sin(ac), jnp.sin(ac)], axis=1)
    tab = jnp.concatenate([cos, sin], axis=1)
    ident = jnp.concatenate([jnp.ones((n_ctx, MLA_ROPE), F32), jnp.zeros((n_ctx, MLA_ROPE), F32)], axis=1)
    return jnp.concatenate([ident, tab], axis=0)


def _swap_pairs(w):
    q = MLA_ROPE // 4
    return jnp.concatenate([w[:, q:2 * q], w[:, :q], w[:, 3 * q:], w[:, 2 * q:3 * q]], axis=1)


def kernel(x, c, ctx, c_ctx, ada_w, ada_b, g_mix, g_ffn, g_final, w_in, g_qn, w_uq, g_kvn, w_ukv,
           na_rel_bias, w_out_attn, w_out_fourier, w_router, w_gate, w_up, w_down):
    _, n, d = x.shape
    n_ctx = ctx.shape[1]
    heads = d // 256
    q_rank = d // 4
    kv_rank = d // 8
    na_width = heads * NA_DIM
    mla_scale = (MLA_NOPE + MLA_ROPE) ** -0.5
    na_scale = NA_DIM ** -0.5
    xs = x[0]
    cx = ctx[0]

    cvecs = jnp.zeros((8, d), F32).at[0].set(c[0]).at[1].set(c_ctx)

    m0 = _ada_terms(cvecs, ada_w, ada_b, 0)
    sh_m, sc_m, g_m, sh_f, sc_f, g_f = [m0[0:1, k * d:(k + 1) * d] for k in range(6)]
    csh_m, csc_m = m0[1:2, 0:d], m0[1:2, d:2 * d]
    gm0 = g_mix[0][None]
    h_x = _modulate(xs, gm0, sh_m, sc_m)
    h_c = _modulate(cx, gm0, csh_m, csc_m)
    h_all = jnp.concatenate([h_c, h_x], axis=0)

    w0 = w_in[0]
    o1, o2, o3 = q_rank, q_rank + kv_rank, q_rank + kv_rank + MLA_ROPE
    w_kr = w0[:, o2:o3]
    w_a = jnp.concatenate([w0[:, :o2], w_kr, _swap_pairs(w_kr)], axis=1).astype(BF16)
    w_q = w0[:, o3:o3 + na_width].astype(BF16)
    w_kv = w0[:, o3 + na_width:].astype(BF16)
    ones = lambda k: jnp.ones((1, k), F32)
    a_all = _matmul(h_all, w_a, ones(w_a.shape[1]), F32)
    q_na = _matmul(h_x, w_q, jnp.full((1, na_width), na_scale * LOG2_E, F32), BF16)
    kv_na = _matmul(h_all, w_kv, ones(2 * na_width), BF16)

    cs = _rope_table(n_ctx, n)
    wq3 = w_uq[0].reshape(q_rank, heads, MLA_NOPE + MLA_ROPE)
    wq_rope = wq3[:, :, MLA_NOPE:]
    wq_swap = jnp.stack([_swap_pairs(wq_rope[:, h]) for h in range(heads)], axis=1)
    wq = jnp.concatenate([wq3, wq_swap], axis=2).transpose(1, 0, 2).astype(BF16)
    wkv = w_ukv[0].reshape(kv_rank, heads, MLA_NOPE + MLA_V).transpose(1, 0, 2).astype(BF16)
    tmp = _pick(math.gcd(n, n_ctx), (256, 128, 64, 8))
    qt_mla = _qprep(a_all, n_ctx // tmp, n, g_qn[0][None], wq, cs, heads, q_rank, mla_scale * LOG2_E, tmp)
    k_mla, vt_mla = _kvprep(a_all, g_kvn[0][None], wkv, cs, heads, q_rank, kv_rank, tmp)
    o_a = _mla_attention(qt_mla, k_mla, vt_mla)

    table = _na_bias_table(na_rel_bias[0])
    o_b = _na_attention(q_na, kv_na, _na_values_transposed(kv_na, heads), table, heads, n_ctx)
    x1 = _matmul_residual(o_a, o_b, w_out_attn[0].astype(BF16), xs, g_m)

    hf, aff = _modulate_router(x1, g_ffn[0][None], sh_f, sc_f, w_router[0])
    moe0 = _ec_moe(hf, aff, w_gate, w_up, w_down, 0)

    m1 = _ada_terms(cvecs, ada_w, ada_b, 1)
    sh_m1, sc_m1, g_m1, sh_f1, sc_f1, g_f1 = [m1[0:1, k * d:(k + 1) * d] for k in range(6)]
    x2, fr = _fourier_transform(x1, g_f, moe0, g_mix[1][None], sh_m1, sc_m1)
    x3 = _matmul_residual(fr, None, w_out_fourier[0].astype(BF16), x2, g_m1)
    hf1, aff1 = _modulate_router(x3, g_ffn[1][None], sh_f1, sc_f1, w_router[1])
    moe1 = _ec_moe(hf1, aff1, w_gate, w_up, w_down, 1)
    out = _final_norm(x3, g_f1, moe1, g_final[None])
    return out[None]
```

```python
import functools
import math

import jax
import jax.numpy as jnp
import numpy as np
from jax import lax
from jax.experimental import pallas as pl
from jax.experimental.pallas import tpu as pltpu

F32 = jnp.float32
BF16 = jnp.bfloat16

GRID_W = 64
NORM_EPS = 1e-6
MLA_NOPE = 128
MLA_ROPE = 64
MLA_V = 128
MLA_QK_PAD = 256
MLA_V_PAD = 144
ROPE_THETA = 10000.0
NA_DIM = 128
NA_WIN_H = 8
NA_WIN_W = 16
NA_ROW_BLOCK = 8
NA_KEY_ROWS = 16
FNET_GROUPS = 8
FFT_ROWS_PER_STEP = 4
N_EXPERTS = 16
EC_CAPACITY = 2
ROUTER_LANES = 128
COMBINE_BLOCK = 256
COMBINE_WIN = 64
COMBINE_ALIGN = 16
NEG_BIG = -1e30
LOG2_E = math.log2(math.e)

V7X_VMEM_LIMIT_BYTES = 56 * 1024 * 1024


def _cparams(sem, flags=None):
    return pltpu.CompilerParams(dimension_semantics=sem, vmem_limit_bytes=V7X_VMEM_LIMIT_BYTES, flags=flags)


def _pick(n, candidates):
    for c in candidates:
        if n % c == 0:
            return c
    return n


def _ada_kernel(c_ref, w_ref, b_ref, o_ref):
    c = c_ref[...]
    s = c / (1.0 + jnp.exp(-c))
    o_ref[...] = jnp.dot(s, w_ref[...], preferred_element_type=F32,
                         precision=lax.Precision.HIGHEST) + b_ref[...]


def _ada_terms(cvecs, w, b, layer):
    depth, d, n6 = w.shape
    tn = _pick(n6, (1024, 512, 256, 128))
    return pl.pallas_call(
        _ada_kernel,
        grid=(n6 // tn,),
        in_specs=[pl.BlockSpec((8, d), lambda j: (0, 0)),
                  pl.BlockSpec((None, d, tn), lambda j: (layer, 0, j)),
                  pl.BlockSpec((None, 1, tn), lambda j: (layer, 0, j))],
        out_specs=pl.BlockSpec((8, tn), lambda j: (0, j)),
        out_shape=jax.ShapeDtypeStruct((8, n6), F32),
        compiler_params=_cparams(("parallel",)),
        name="ada_terms",
    )(cvecs, w, b.reshape(depth, 1, n6))


def _modulated(x, g, shift, scale):
    ms = jnp.mean(x * x, axis=-1, keepdims=True)
    y = x * lax.rsqrt(ms + NORM_EPS) * g
    return y * (1.0 + scale) + shift


def _mod_kernel(x_ref, g_ref, sh_ref, sc_ref, h_ref):
    h_ref[...] = _modulated(x_ref[...], g_ref[...], sh_ref[...], sc_ref[...]).astype(h_ref.dtype)


def _modulate(x, g, shift, scale, out_dtype=BF16):
    n, d = x.shape
    tm = _pick(n, (512, 256, 128, 64, 8))
    row = pl.BlockSpec((tm, d), lambda i: (i, 0))
    vec = pl.BlockSpec((1, d), lambda i: (0, 0))
    return pl.pallas_call(
        _mod_kernel,
        grid=(n // tm,),
        in_specs=[row, vec, vec, vec],
        out_specs=row,
        out_shape=jax.ShapeDtypeStruct((n, d), out_dtype),
        compiler_params=_cparams(("parallel",)),
        name="modulate",
    )(x, g, shift, scale)


def _router_kernel(n_experts, x_ref, g_ref, sh_ref, sc_ref, whi_ref, wlo_ref, h_ref, aff_ref):
    h = _modulated(x_ref[...], g_ref[...], sh_ref[...], sc_ref[...])
    h_hi = h.astype(BF16)
    h_ref[...] = h_hi
    h_lo = (h - h_hi.astype(F32)).astype(BF16)
    logits = (jnp.dot(h_hi, whi_ref[...], preferred_element_type=F32)
              + jnp.dot(h_hi, wlo_ref[...], preferred_element_type=F32)
              + jnp.dot(h_lo, whi_ref[...], preferred_element_type=F32))
    lane = lax.broadcasted_iota(jnp.int32, logits.shape, 1)
    logits = jnp.where(lane < n_experts, logits, NEG_BIG)
    m = jnp.max(logits, axis=-1, keepdims=True)
    p = jnp.exp(logits - m)
    aff_ref[...] = p / jnp.sum(p, axis=-1, keepdims=True)


def _modulate_router(x, g, shift, scale, w_router):
    n, d = x.shape
    n_experts = w_router.shape[1]
    wr = jnp.zeros((d, ROUTER_LANES), F32).at[:, :n_experts].set(w_router)
    wr_hi = wr.astype(BF16)
    wr_lo = (wr - wr_hi.astype(F32)).astype(BF16)
    tm = _pick(n, (512, 256, 128, 64, 8))
    row = pl.BlockSpec((tm, d), lambda i: (i, 0))
    vec = pl.BlockSpec((1, d), lambda i: (0, 0))
    wspec = pl.BlockSpec((d, ROUTER_LANES), lambda i: (0, 0))
    return pl.pallas_call(
        functools.partial(_router_kernel, n_experts),
        grid=(n // tm,),
        in_specs=[row, vec, vec, vec, wspec, wspec],
        out_specs=[row, pl.BlockSpec((tm, ROUTER_LANES), lambda i: (i, 0))],
        out_shape=[jax.ShapeDtypeStruct((n, d), BF16),
                   jax.ShapeDtypeStruct((n, ROUTER_LANES), F32)],
        compiler_params=_cparams(("parallel",)),
        name="modulate_router",
    )(x, g, shift, scale, wr_hi, wr_lo)


def _resid_kernel(x_ref, gate_ref, y_ref, o_ref):
    o_ref[...] = x_ref[...] + gate_ref[...] * y_ref[...]


def _residual(x, gate, y):
    n, d = x.shape
    tm = _pick(n, (512, 256, 128, 64, 8))
    row = pl.BlockSpec((tm, d), lambda i: (i, 0))
    vec = pl.BlockSpec((1, d), lambda i: (0, 0))
    return pl.pallas_call(
        _resid_kernel,
        grid=(n // tm,),
        in_specs=[row, vec, row],
        out_specs=row,
        out_shape=jax.ShapeDtypeStruct((n, d), F32),
        compiler_params=_cparams(("parallel",)),
        name="residual",
    )(x, gate, y)


def _final_kernel(x_ref, gate_ref, y_ref, g_ref, o_ref):
    x = x_ref[...] + gate_ref[...] * y_ref[...]
    ms = jnp.mean(x * x, axis=-1, keepdims=True)
    o_ref[...] = x * lax.rsqrt(ms + NORM_EPS) * g_ref[...]


def _final_norm(x, gate, y, g):
    n, d = x.shape
    tm = _pick(n, (512, 256, 128, 64, 8))
    row = pl.BlockSpec((tm, d), lambda i: (i, 0))
    vec = pl.BlockSpec((1, d), lambda i: (0, 0))
    return pl.pallas_call(
        _final_kernel,
        grid=(n // tm,),
        in_specs=[row, vec, row, vec],
        out_specs=row,
        out_shape=jax.ShapeDtypeStruct((n, d), F32),
        compiler_params=_cparams(("parallel",)),
        name="final_norm",
    )(x, gate, y, g)


def _mm_kernel(a_ref, w_ref, cs_ref, o_ref):
    acc = jnp.dot(a_ref[...], w_ref[...], preferred_element_type=F32)
    o_ref[...] = (acc * cs_ref[...]).astype(o_ref.dtype)


def _mm_resid_kernel(a1_ref, a2_ref, w_ref, x_ref, gate_ref, o_ref):
    k1 = a1_ref.shape[1]
    acc = jnp.dot(a1_ref[...], w_ref[:k1, :], preferred_element_type=F32)
    acc = acc + jnp.dot(a2_ref[...], w_ref[k1:, :], preferred_element_type=F32)
    o_ref[...] = x_ref[...] + gate_ref[...] * acc


def _mm_resid_f32_kernel(a_ref, w_ref, x_ref, gate_ref, o_ref):
    acc = jnp.dot(a_ref[...].astype(BF16), w_ref[...], preferred_element_type=F32)
    o_ref[...] = x_ref[...] + gate_ref[...] * acc


def _mm_tiles(m, n):
    tm = _pick(m, (1280, 1024, 512, 256, 128, 64, 8))
    tn = n if n <= 1024 else _pick(n, (1024, 512, 256, 128))
    return tm, tn


def _matmul(a, w, col_scale, out_dtype):
    m, k = a.shape
    n = w.shape[1]
    tm, tn = _mm_tiles(m, n)
    return pl.pallas_call(
        _mm_kernel,
        grid=(m // tm, n // tn),
        in_specs=[pl.BlockSpec((tm, k), lambda i, j: (i, 0)),
                  pl.BlockSpec((k, tn), lambda i, j: (0, j)),
                  pl.BlockSpec((1, tn), lambda i, j: (0, j))],
        out_specs=pl.BlockSpec((tm, tn), lambda i, j: (i, j)),
        out_shape=jax.ShapeDtypeStruct((m, n), out_dtype),
        compiler_params=_cparams(("parallel", "parallel")),
        name="matmul",
    )(a, w, col_scale)


def _matmul_residual(a1, a2, w, x, gate):
    m = a1.shape[0]
    n = w.shape[1]
    k1 = w.shape[0] // 2
    tm, tn = _mm_tiles(m, n)
    a_spec = pl.BlockSpec((tm, k1), lambda i, j: (i, 0))
    operands, a_specs, body = (a1, a2), [a_spec, a_spec], _mm_resid_kernel
    if a2 is None:
        operands, a_specs, body = (a1,), [pl.BlockSpec((tm, 2 * k1), lambda i, j: (i, 0))], _mm_resid_f32_kernel
    return pl.pallas_call(
        body,
        grid=(m // tm, n // tn),
        in_specs=a_specs + [pl.BlockSpec((2 * k1, tn), lambda i, j: (0, j)),
                            pl.BlockSpec((tm, tn), lambda i, j: (i, j)),
                            pl.BlockSpec((1, tn), lambda i, j: (0, j))],
        out_specs=pl.BlockSpec((tm, tn), lambda i, j: (i, j)),
        out_shape=jax.ShapeDtypeStruct((m, n), F32),
        compiler_params=_cparams(("parallel", "parallel")),
        name="matmul_residual",
    )(*operands, w, x, gate)


def _rms(x, g):
    ms = jnp.mean(x * x, axis=-1, keepdims=True)
    return x * lax.rsqrt(ms + NORM_EPS) * g


def _rope_half(pair, cs):
    t = pair * cs
    t = t + pltpu.roll(t, MLA_ROPE, 1)
    lane = lax.broadcasted_iota(jnp.int32, t.shape, 1)
    return jnp.where(lane < MLA_ROPE, t, 0.0)


def _qprep_kernel(heads, q_rank, scale, a_ref, g_ref, w_ref, cs_ref, q_ref):
    cq = _rms(a_ref[:, :q_rank], g_ref[...]).astype(BF16)
    cs = cs_ref[...]
    for h in range(heads):
        qh = jnp.dot(cq, w_ref[h], preferred_element_type=F32)
        q = jnp.concatenate([qh[:, :MLA_NOPE], _rope_half(qh[:, MLA_NOPE:], cs)], axis=1) * scale
        q_ref[h] = q.T.astype(BF16)


def _kvprep_kernel(heads, q_rank, kv_rank, a_ref, g_ref, w_ref, cs_ref, k_ref, v_ref):
    ckv = _rms(a_ref[:, q_rank:q_rank + kv_rank], g_ref[...]).astype(BF16)
    kr = _rope_half(a_ref[:, q_rank + kv_rank:], cs_ref[...]).astype(BF16)
    row = lax.broadcasted_iota(jnp.int32, (MLA_V_PAD - MLA_V, a_ref.shape[0]), 0)
    ones_rows = jnp.where(row == 0, 1.0, 0.0)
    for h in range(heads):
        kvh = jnp.dot(ckv, w_ref[h], preferred_element_type=F32)
        k_ref[h, :, :MLA_NOPE] = kvh[:, :MLA_NOPE].astype(BF16)
        k_ref[h, :, MLA_NOPE:] = kr
        vt = kvh[:, MLA_NOPE:].T
        v_ref[h] = jnp.concatenate([vt, ones_rows], axis=0).astype(BF16)


def _qprep(a, row_off_blocks, n_rows, g_qn, wq, cs, heads, q_rank, scale, tm):
    aw = a.shape[1]
    return pl.pallas_call(
        functools.partial(_qprep_kernel, heads, q_rank, scale),
        grid=(n_rows // tm,),
        in_specs=[pl.BlockSpec((tm, aw), lambda i: (i + row_off_blocks, 0)),
                  pl.BlockSpec((1, q_rank), lambda i: (0, 0)),
                  pl.BlockSpec((heads, q_rank, MLA_QK_PAD), lambda i: (0, 0, 0)),
                  pl.BlockSpec((tm, 2 * MLA_ROPE), lambda i: (i + row_off_blocks, 0))],
        out_specs=pl.BlockSpec((heads, MLA_QK_PAD, tm), lambda i: (0, 0, i)),
        out_shape=jax.ShapeDtypeStruct((heads, MLA_QK_PAD, n_rows), BF16),
        compiler_params=_cparams(("parallel",)),
        name="mla_q_prep",
    )(a, g_qn, wq, cs)


def _kvprep(a, g_kvn, wkv, cs, heads, q_rank, kv_rank, tm):
    n_rows, aw = a.shape
    return pl.pallas_call(
        functools.partial(_kvprep_kernel, heads, q_rank, kv_rank),
        grid=(n_rows // tm,),
        in_specs=[pl.BlockSpec((tm, aw), lambda i: (i, 0)),
                  pl.BlockSpec((1, kv_rank), lambda i: (0, 0)),
                  pl.BlockSpec((heads, kv_rank, MLA_NOPE + MLA_V), lambda i: (0, 0, 0)),
                  pl.BlockSpec((tm, 2 * MLA_ROPE), lambda i: (i, 0))],
        out_specs=[pl.BlockSpec((heads, tm, MLA_QK_PAD), lambda i: (0, i, 0)),
                   pl.BlockSpec((heads, MLA_V_PAD, tm), lambda i: (0, 0, i))],
        out_shape=[jax.ShapeDtypeStruct((heads, n_rows, MLA_QK_PAD), BF16),
                   jax.ShapeDtypeStruct((heads, MLA_V_PAD, n_rows), BF16)],
        compiler_params=_cparams(("parallel",)),
        name="mla_kv_prep",
    )(a, g_kvn, wkv, cs)


def _dot_nt(a, b):
    return lax.dot_general(a, b, (((1,), (1,)), ((), ())), preferred_element_type=F32)


def _mla_kernel(tk, n_kv, n_sub, qt_ref, k_ref, vt_ref, o_ref, s_ref, p_ref, acc_ref):
    tq = qt_ref.shape[2] // n_sub
    n_chunks = n_kv // tk

    def trip(t, par, carry, do_a=True, do_b=True, do_c=True):
        ms, al_new, al_old = carry
        out_ms, out_al = ms, al_new
        if do_a:
            start = t * tk if isinstance(t, int) else pl.multiple_of(t * tk, tk)
            kc = k_ref[0, pl.ds(start, tk), :]
            out_ms, out_al = [], []
            for u in range(n_sub):
                s = jnp.dot(kc, qt_ref[0, :, u * tq:(u + 1) * tq], preferred_element_type=F32)
                s_ref[par, u] = s
                m_new = jnp.maximum(ms[u], jnp.max(s, axis=0, keepdims=True))
                out_al.append(jnp.exp2(ms[u] - m_new))
                out_ms.append(m_new)
            out_ms, out_al = tuple(out_ms), tuple(out_al)
        if do_b:
            for u in range(n_sub):
                p_ref[1 - par, u] = jnp.exp2((s_ref[1 - par, u] - ms[u]).astype(BF16))
        if do_c:
            start = (t - 2) * tk if isinstance(t, int) else pl.multiple_of((t - 2) * tk, tk)
            vc = vt_ref[0, :, pl.ds(start, tk)]
            for u in range(n_sub):
                acc_ref[u] = al_old[u] * acc_ref[u] + jnp.dot(vc, p_ref[par, u], preferred_element_type=F32)
        return out_ms, out_al, al_new

    acc_ref[...] = jnp.zeros_like(acc_ref)
    neg = tuple(jnp.full((1, tq), NEG_BIG, F32) for _ in range(n_sub))
    zero = tuple(jnp.zeros((1, tq), F32) for _ in range(n_sub))
    carry = (neg, zero, zero)
    carry = trip(0, 0, carry, do_b=False, do_c=False)
    if n_chunks > 1:
        carry = trip(1, 1, carry, do_c=False)
    n_steady = max(n_chunks - 2, 0)

    def double_trip(i, carry):
        t = 2 + 2 * i
        return trip(t + 1, 1, trip(t, 0, carry))

    carry = lax.fori_loop(0, n_steady // 2, double_trip, carry)
    if n_steady % 2 == 1:
        carry = trip(n_chunks - 1, (n_chunks - 1) % 2, carry)
    carry = trip(n_chunks, n_chunks % 2, carry, do_a=False, do_c=n_chunks > 1)
    if n_chunks > 1:
        carry = trip(n_chunks + 1, (n_chunks + 1) % 2, carry, do_a=False, do_b=False)
    else:
        carry = trip(2, 0, carry, do_a=False, do_b=False)
    for u in range(n_sub):
        o = acc_ref[u, :MLA_V, :] / acc_ref[u, MLA_V:MLA_V + 1, :]
        o_ref[u * tq:(u + 1) * tq, :] = o.T.astype(o_ref.dtype)


def _mla_attention(qt, k, vt):
    heads, _, n = qt.shape
    n_kv = k.shape[1]
    tq = _pick(n, (256, 128))
    n_sub = 4 if n % (4 * tq) == 0 else 1
    tk = _pick(n_kv, (1280, 768, 512, 256, 128))
    return pl.pallas_call(
        functools.partial(_mla_kernel, tk, n_kv, n_sub),
        grid=(heads, n // (tq * n_sub)),
        in_specs=[pl.BlockSpec((1, MLA_QK_PAD, tq * n_sub), lambda h, i: (h, 0, i)),
                  pl.BlockSpec((1, n_kv, MLA_QK_PAD), lambda h, i: (h, 0, 0)),
                  pl.BlockSpec((1, MLA_V_PAD, n_kv), lambda h, i: (h, 0, 0))],
        out_specs=pl.BlockSpec((tq * n_sub, MLA_V), lambda h, i: (i, h)),
        out_shape=jax.ShapeDtypeStruct((n, heads * MLA_V), BF16),
        scratch_shapes=[pltpu.VMEM((2, n_sub, tk, tq), F32),
                        pltpu.VMEM((2, n_sub, tk, tq), BF16),
                        pltpu.VMEM((n_sub, MLA_V_PAD, tq), F32)],
        compiler_params=_cparams(("parallel", "parallel")),
        name="mla_attention",
    )(qt, k, vt)


def _na_kernel(n_ctx, rows_total, win_h, key_rows, q_ref, k_ref, vt_ref, t_ref, o_ref,
               bm_ref, s_ref, p_ref, m_ref):
    hd = pl.program_id(0)
    b = pl.program_id(1)
    nb = rows_total // NA_ROW_BLOCK
    n_loc = key_rows * GRID_W
    par = b % 2

    def window_start(blk):
        base = jnp.clip(blk * NA_ROW_BLOCK - NA_WIN_H // 2, 0, rows_total - key_rows)
        return pl.multiple_of(n_ctx + base * GRID_W, 128)

    @pl.when((hd == 0) & (b == 0))
    def _():
        s_ref[...] = jnp.zeros_like(s_ref)
        p_ref[...] = jnp.ones_like(p_ref)
        m_ref[...] = jnp.zeros_like(m_ref)

    @pl.when(b == 0)
    def _():
        for v, blk in enumerate((0, min(1, nb - 1), nb - 1)):
            r0 = blk * NA_ROW_BLOCK
            base = min(max(r0 - NA_WIN_H // 2, 0), rows_total - key_rows)
            for i in range(NA_ROW_BLOCK):
                rs = min(max(r0 + i - win_h // 2, 0), rows_total - win_h)
                for j in range(key_rows):
                    kr = base + j
                    tile = (slice(j * GRID_W, (j + 1) * GRID_W), slice(i * GRID_W, (i + 1) * GRID_W))
                    if rs <= kr < rs + win_h:
                        bm_ref[(v,) + tile] = t_ref[0, kr - (r0 + i) + NA_WIN_H - 1]
                    else:
                        bm_ref[(v,) + tile] = jnp.full((GRID_W, GRID_W), NEG_BIG, F32)

    def stages(par):
        ta = jnp.minimum(b, nb - 1)
        variant = jnp.where(ta == 0, 0, jnp.where(ta == nb - 1, 2, 1))
        q = q_ref[...]
        s_ctx = _dot_nt(k_ref[0:n_ctx, :], q)
        s_loc = _dot_nt(k_ref[pl.ds(window_start(ta), n_loc), :], q) + bm_ref[variant]
        s_ref[par, 0:n_ctx, :] = s_ctx
        s_ref[par, n_ctx:, :] = s_loc
        m_ref[par] = jnp.maximum(jnp.max(s_ctx, axis=0, keepdims=True), jnp.max(s_loc, axis=0, keepdims=True))

        p_ref[1 - par] = jnp.exp2((s_ref[1 - par] - m_ref[1 - par]).astype(BF16))

        tc = jnp.clip(b - 2, 0, nb - 1)
        acc = jnp.dot(vt_ref[0, :, 0:n_ctx], p_ref[par, 0:n_ctx, :], preferred_element_type=F32)
        acc = acc + jnp.dot(vt_ref[0, :, pl.ds(window_start(tc), n_loc)], p_ref[par, n_ctx:, :],
                            preferred_element_type=F32)
        o = acc[:NA_DIM] / acc[NA_DIM:NA_DIM + 1]
        o_ref[...] = o.T.astype(o_ref.dtype)

    for static_par in (0, 1):
        pl.when(par == static_par)(functools.partial(stages, static_par))


def _na_bias_table(rel_bias):
    cq = jnp.arange(GRID_W)
    ck = jnp.arange(GRID_W)
    col_start = jnp.clip(cq - NA_WIN_W // 2, 0, GRID_W - NA_WIN_W)
    valid = (ck[:, None] >= col_start[None, :]) & (ck[:, None] < col_start[None, :] + NA_WIN_W)
    co = jnp.clip(ck[:, None] - cq[None, :] + NA_WIN_W - 1, 0, 2 * NA_WIN_W - 2)
    onehot = (co[None] == jnp.arange(2 * NA_WIN_W - 1)[:, None, None]).astype(F32)
    t = jnp.einsum('hrc,ckq->hrkq', rel_bias, onehot, precision=lax.Precision.HIGHEST) * LOG2_E
    return jnp.where(valid[None, None], t, NEG_BIG)


def _vt_prep_kernel(heads, x_ref, o_ref):
    row = lax.broadcasted_iota(jnp.int32, (MLA_V_PAD - NA_DIM, x_ref.shape[0]), 0)
    ones_rows = jnp.where(row == 0, 1.0, 0.0)
    for h in range(heads):
        vt = x_ref[:, h * NA_DIM:(h + 1) * NA_DIM].astype(F32).T
        o_ref[h] = jnp.concatenate([vt, ones_rows], axis=0).astype(BF16)


def _na_values_transposed(kv, heads):
    n_kv = kv.shape[0]
    tm = _pick(n_kv, (256, 128))
    return pl.pallas_call(
        functools.partial(_vt_prep_kernel, heads),
        grid=(n_kv // tm,),
        in_specs=[pl.BlockSpec((tm, heads * NA_DIM), lambda i: (i, 1))],
        out_specs=pl.BlockSpec((heads, MLA_V_PAD, tm), lambda i: (0, 0, i)),
        out_shape=jax.ShapeDtypeStruct((heads, MLA_V_PAD, n_kv), BF16),
        compiler_params=_cparams(("parallel",)),
        name="na_v_prep",
    )(kv)


def _na_attention(q, kv, vt, table, heads, n_ctx):
    n = q.shape[0]
    n_kv = kv.shape[0]
    rows_total = n // GRID_W
    win_h = min(NA_WIN_H, rows_total)
    key_rows = min(NA_KEY_ROWS, rows_total)
    tq = NA_ROW_BLOCK * GRID_W
    nb = n // tq
    n_loc = key_rows * GRID_W
    n_off = table.shape[1]
    return pl.pallas_call(
        functools.partial(_na_kernel, n_ctx, rows_total, win_h, key_rows),
        grid=(heads, nb + 2),
        in_specs=[pl.BlockSpec((tq, NA_DIM), lambda h, i: (jnp.minimum(i, nb - 1), h)),
                  pl.BlockSpec((n_kv, NA_DIM), lambda h, i: (0, h)),
                  pl.BlockSpec((1, MLA_V_PAD, n_kv), lambda h, i: (h, 0, 0)),
                  pl.BlockSpec((1, n_off, GRID_W, GRID_W), lambda h, i: (h, 0, 0, 0))],
        out_specs=pl.BlockSpec((tq, NA_DIM), lambda h, i: (jnp.clip(i - 2, 0, nb - 1), h)),
        out_shape=jax.ShapeDtypeStruct((n, heads * NA_DIM), BF16),
        scratch_shapes=[pltpu.VMEM((3, n_loc, tq), F32),
                        pltpu.VMEM((2, n_ctx + n_loc, tq), F32),
                        pltpu.VMEM((2, n_ctx + n_loc, tq), BF16),
                        pltpu.VMEM((2, 1, tq), F32)],
        compiler_params=_cparams(("arbitrary", "arbitrary")),
        name="na_attention",
    )(q, kv, vt, table)


def _fft_a_kernel(groups, x_ref, gate_ref, y_ref, g_ref, sh_ref, sc_ref, w_ref, xo_ref, z_ref):
    x = x_ref[...] + gate_ref[...] * y_ref[...]
    xo_ref[...] = x
    h = _modulated(x, g_ref[...], sh_ref[...], sc_ref[...]).astype(BF16)
    d = h.shape[1]
    dg = d // groups
    for g in range(groups):
        z = jnp.dot(h[:, g * dg:(g + 1) * dg], w_ref[...], preferred_element_type=F32)
        z_ref[:, g * dg:(g + 1) * dg] = z[:, :dg]
        z_ref[:, d + g * dg:d + (g + 1) * dg] = z[:, dg:]


def _strided_fetch(src_hbm, buf, sem, step, slot, rows):
    return [pltpu.make_async_copy(src_hbm.at[:, step * rows + r, :], buf.at[slot, r], sem.at[slot, r])
            for r in range(rows)]


def _prefetch_next(src_hbm, buf, sem, rows):
    j = pl.program_id(0)
    slot = j % 2

    @pl.when(j == 0)
    def _():
        for c in _strided_fetch(src_hbm, buf, sem, 0, 0, rows):
            c.start()

    @pl.when(j + 1 < pl.num_programs(0))
    def _():
        for c in _strided_fetch(src_hbm, buf, sem, j + 1, 1 - slot, rows):
            c.start()

    return j, slot


def _fft_b_kernel(z_hbm, w_ref, y_ref, buf, sem):
    rows, n1 = y_ref.shape[0], y_ref.shape[1]
    d = y_ref.shape[2] // 2
    j, slot = _prefetch_next(z_hbm, buf, sem, rows)
    waits = _strided_fetch(z_hbm, buf, sem, j, slot, rows)
    for r in range(rows):
        waits[r].wait()
        z = jnp.concatenate([buf[slot, r, :, :d], buf[slot, r, :, d:]], axis=0).astype(BF16)
        y = jnp.dot(w_ref[...], z, preferred_element_type=F32)
        y_ref[r, :, :d] = y[:n1]
        y_ref[r, :, d:] = y[n1:]


def _fft_c_kernel(y_hbm, w_ref, f_hbm, buf, obuf, sem, osem):
    rows = buf.shape[1]
    d = obuf.shape[2]
    j, slot = _prefetch_next(y_hbm, buf, sem, rows)
    waits = _strided_fetch(y_hbm, buf, sem, j, slot, rows)
    puts = [pltpu.make_async_copy(obuf.at[r], f_hbm.at[:, j * rows + r, :], osem.at[r]) for r in range(rows)]
    for r in range(rows):
        waits[r].wait()
        y = jnp.concatenate([buf[slot, r, :, :d], buf[slot, r, :, d:]], axis=0).astype(BF16)
        obuf[r] = jnp.dot(w_ref[r], y, preferred_element_type=F32)
        puts[r].start()
    for r in range(rows):
        puts[r].wait()


def _dft_tables(n, dg):
    lg = int(math.log2(n))
    n1 = 1 << ((lg + 1) // 2)
    n2 = n // n1
    scale_a = 1.0 / math.sqrt(dg)
    sb = 1 << (int(math.log2(n1)) // 2)
    scale_b = 1.0 / sb
    scale_c = 1.0 / (math.sqrt(n) / sb)
    ch = np.arange(dg)
    ang = (2.0 * np.pi / dg) * ((ch[:, None] * ch[None, :]) % dg)
    wa = np.concatenate([np.cos(ang), -np.sin(ang)], axis=1) * scale_a
    a = np.arange(n1)
    ang = (2.0 * np.pi / n1) * ((a[:, None] * a[None, :]) % n1)
    c, s = np.cos(ang), np.sin(ang)
    wb = np.concatenate([np.concatenate([c, s], axis=1), np.concatenate([-s, c], axis=1)], axis=0) * scale_b
    k1 = np.arange(n1)[:, None, None]
    k2 = np.arange(n2)[None, :, None]
    b = np.arange(n2)[None, None, :]
    ang = (2.0 * np.pi / n) * ((b * (k1 + n1 * k2)) % n)
    wc = np.concatenate([np.cos(ang), np.sin(ang)], axis=2) * scale_c
    as_bf16 = lambda t: jnp.asarray(t.astype(np.float32)).astype(BF16)
    return n1, n2, as_bf16(wa), as_bf16(wb), as_bf16(wc)


def _fourier_transform(x, gate, y, g, shift, scale):
    n, d = x.shape
    dg = d // FNET_GROUPS
    n1, n2, wa, wb, wc = _dft_tables(n, dg)
    tm = _pick(n, (512, 256, 128, 64, 8))
    row = pl.BlockSpec((tm, d), lambda i: (i, 0))
    vec = pl.BlockSpec((1, d), lambda i: (0, 0))
    x_new, z = pl.pallas_call(
        functools.partial(_fft_a_kernel, FNET_GROUPS),
        grid=(n // tm,),
        in_specs=[row, vec, row, vec, vec, vec, pl.BlockSpec((dg, 2 * dg), lambda i: (0, 0))],
        out_specs=[row, pl.BlockSpec((tm, 2 * d), lambda i: (i, 0))],
        out_shape=[jax.ShapeDtypeStruct((n, d), F32),
                   jax.ShapeDtypeStruct((n, 2 * d), F32)],
        compiler_params=_cparams(("parallel",)),
        name="fft_channels",
    )(x, gate, y, g, shift, scale, wa)

    gb = _pick(n2, (FFT_ROWS_PER_STEP,))
    yw = pl.pallas_call(
        _fft_b_kernel,
        grid=(n2 // gb,),
        in_specs=[pl.BlockSpec(memory_space=pl.ANY), pl.BlockSpec((2 * n1, 2 * n1), lambda j: (0, 0))],
        out_specs=pl.BlockSpec((gb, n1, 2 * d), lambda j: (j, 0, 0)),
        out_shape=jax.ShapeDtypeStruct((n2, n1, 2 * d), F32),
        scratch_shapes=[pltpu.VMEM((2, gb, n1, 2 * d), F32), pltpu.SemaphoreType.DMA((2, gb))],
        compiler_params=_cparams(("arbitrary",)),
        name="fft_pos_outer",
    )(z.reshape(n1, n2, 2 * d), wb)

    gc = _pick(n1, (FFT_ROWS_PER_STEP,))
    f = pl.pallas_call(
        _fft_c_kernel,
        grid=(n1 // gc,),
        in_specs=[pl.BlockSpec(memory_space=pl.ANY), pl.BlockSpec((gc, n2, 2 * n2), lambda k: (k, 0, 0))],
        out_specs=pl.BlockSpec(memory_space=pl.ANY),
        out_shape=jax.ShapeDtypeStruct((n2, n1, d), F32),
        scratch_shapes=[pltpu.VMEM((2, gc, n2, 2 * d), F32), pltpu.VMEM((gc, n2, d), F32),
                        pltpu.SemaphoreType.DMA((2, gc)), pltpu.SemaphoreType.DMA((gc,))],
        compiler_params=_cparams(("arbitrary",)),
        name="fft_pos_inner",
    )(yw, wc)
    return x_new, f.reshape(n, d)


def _expert_kernel(x_ref, wg_ref, wu_ref, wd_ref, gate_ref, o_ref, acc_ref):
    f = pl.program_id(2)

    @pl.when(f == 0)
    def _():
        acc_ref[...] = jnp.zeros_like(acc_ref)

    x = x_ref[0]
    a = jnp.dot(x, wg_ref[...].astype(BF16), preferred_element_type=F32)
    u = jnp.dot(x, wu_ref[...].astype(BF16), preferred_element_type=F32)
    hmid = (a / (1.0 + jnp.exp(-a)) * u).astype(BF16)
    acc_ref[...] += jnp.dot(hmid, wd_ref[...].astype(BF16), preferred_element_type=F32)

    @pl.when(f == pl.num_programs(2) - 1)
    def _():
        o_ref[0] = (acc_ref[...] * gate_ref[0]).astype(o_ref.dtype)


def _expert_ffn(xs, w_gate, w_up, w_down, gates, layer):
    e, cap, d = xs.shape
    ff = w_gate.shape[3]
    tm = _pick(cap, (1024, 512, 256, 128, 64, 8))
    tf = _pick(ff, (256, 128))
    return pl.pallas_call(
        _expert_kernel,
        grid=(e, cap // tm, ff // tf),
        in_specs=[pl.BlockSpec((1, tm, d), lambda ei, i, f: (ei, i, 0)),
                  pl.BlockSpec((None, None, d, tf), lambda ei, i, f: (layer, ei, 0, f)),
                  pl.BlockSpec((None, None, d, tf), lambda ei, i, f: (layer, ei, 0, f)),
                  pl.BlockSpec((None, None, tf, d), lambda ei, i, f: (layer, ei, f, 0)),
                  pl.BlockSpec((1, tm, 1), lambda ei, i, f: (ei, i, 0))],
        out_specs=pl.BlockSpec((1, tm, d), lambda ei, i, f: (ei, i, 0)),
        out_shape=jax.ShapeDtypeStruct((e, cap, d), BF16),
        scratch_shapes=[pltpu.VMEM((tm, d), F32)],
        compiler_params=_cparams(("parallel", "parallel", "arbitrary")),
        name="expert_ffn",
    )(xs, w_gate, w_up, w_down, gates)


def _combine_kernel(n_exp, cap, w0_ref, npass_ref, slot_ref, w0v_ref, y_hbm, r_ref, o_ref, buf_ref, acc_ref, sem):
    b = pl.program_id(0)
    nb = pl.num_programs(0)
    win = COMBINE_WIN
    par = b % 2
    slot = slot_ref[...]
    w0v = w0v_ref[...]
    jio = (lax.broadcasted_iota(jnp.int32, (1, n_exp * win), 1) % win).astype(F32)

    def window_copies(blk, p, buf_slot):
        copies = []
        for e in range(n_exp):
            st = pl.multiple_of(jnp.minimum(w0_ref[blk * n_exp + e] + p * win, cap - win), COMBINE_ALIGN)
            copies.append(pltpu.make_async_copy(y_hbm.at[e, pl.ds(st, win), :],
                                                buf_ref.at[buf_slot, pl.ds(e * win, win), :],
                                                sem.at[buf_slot, e]))
        return copies

    def onehot_of(p):
        lo = w0v + p * win
        st_vec = jnp.minimum(lo, cap - win)
        valid = (slot >= lo) & (slot < lo + win)
        rel = jnp.where(valid, slot - st_vec, -1).astype(F32).astype(BF16)
        relx = jnp.dot(rel, r_ref[...], preferred_element_type=F32)
        return jnp.where(relx == jio, 1.0, 0.0).astype(BF16)

    @pl.when(b == 0)
    def _():
        for c in window_copies(0, 0, 0):
            c.start()

    @pl.when(b + 1 < nb)
    def _():
        for c in window_copies(b + 1, 0, 1 - par):
            c.start()

    onehot = onehot_of(0)
    for c in window_copies(b, 0, par):
        c.wait()
    acc_ref[...] = jnp.dot(onehot, buf_ref[par], preferred_element_type=F32)

    def extra_pass(p, carry):
        copies = window_copies(b, p, par)
        for c in copies:
            c.start()
        onehot = onehot_of(p)
        for c in copies:
            c.wait()
        acc_ref[...] += jnp.dot(onehot, buf_ref[par], preferred_element_type=F32)
        return carry

    lax.fori_loop(1, npass_ref[b], extra_pass, 0)
    o_ref[...] = acc_ref[...].astype(o_ref.dtype)


def _combine(y, slot, w0, npass, n):
    n_exp, cap, d = y.shape
    tb = _pick(n, (COMBINE_BLOCK,))
    nb = n // tb
    win = COMBINE_WIN
    rows = lax.broadcasted_iota(jnp.int32, (ROUTER_LANES, n_exp * win), 0)
    cols = lax.broadcasted_iota(jnp.int32, (ROUTER_LANES, n_exp * win), 1)
    spread = (cols // win == rows).astype(BF16)
    w0v = jnp.zeros((nb, 1, ROUTER_LANES), jnp.int32).at[:, 0, :n_exp].set(w0)
    grid_spec = pltpu.PrefetchScalarGridSpec(
        num_scalar_prefetch=2,
        grid=(nb,),
        in_specs=[pl.BlockSpec((tb, ROUTER_LANES), lambda b, *_: (b, 0)),
                  pl.BlockSpec((None, 1, ROUTER_LANES), lambda b, *_: (b, 0, 0)),
                  pl.BlockSpec(memory_space=pl.ANY),
                  pl.BlockSpec((ROUTER_LANES, n_exp * win), lambda b, *_: (0, 0))],
        out_specs=pl.BlockSpec((tb, d), lambda b, *_: (b, 0)),
        scratch_shapes=[pltpu.VMEM((2, n_exp * win, d), BF16),
                        pltpu.VMEM((tb, d), F32),
                        pltpu.SemaphoreType.DMA((2, n_exp))])
    return pl.pallas_call(
        functools.partial(_combine_kernel, n_exp, cap),
        grid_spec=grid_spec,
        out_shape=jax.ShapeDtypeStruct((n, d), BF16),
        compiler_params=_cparams(("arbitrary",)),
        name="moe_combine",
    )(w0.reshape(-1), npass, slot, w0v, y, spread)


def _ec_moe(hf, aff, w_gate, w_up, w_down, layer):
    n, d = hf.shape
    e = w_gate.shape[1]
    cap = EC_CAPACITY * n // e
    aff_e = aff[:, :e]
    gates, idx = lax.top_k(aff_e.T, cap)
    idx_s, gates_s = lax.sort((idx, gates), dimension=1, num_keys=1)
    xs = jnp.take(hf, idx_s.reshape(-1), axis=0).reshape(e, cap, d)
    y = _expert_ffn(xs, w_gate, w_up, w_down, gates_s[..., None], layer)

    thr = gates[:, cap - 1][None, :]
    gt = aff_e > thr
    eq = aff_e == thr
    need = cap - jnp.sum(gt, axis=0, keepdims=True)
    eq_i = eq.astype(jnp.int32)
    sel = gt | (eq & (jnp.cumsum(eq_i, axis=0) - eq_i < need))
    sel_i = sel.astype(jnp.int32)
    pos = jnp.cumsum(sel_i, axis=0) - sel_i
    slot = jnp.full((n, ROUTER_LANES), -1, jnp.int32).at[:, :e].set(jnp.where(sel, pos, -1))
    tb = _pick(n, (COMBINE_BLOCK,))
    off = pos[::tb]
    cnt = jnp.sum(sel_i.reshape(n // tb, tb, e), axis=1)
    w0 = jnp.minimum(off // COMBINE_ALIGN * COMBINE_ALIGN, cap - COMBINE_WIN)
    npass = jnp.max(jnp.where(cnt > 0, (off + cnt - w0 + COMBINE_WIN - 1) // COMBINE_WIN, 0), axis=1)
    return _combine(y, slot, w0.astype(jnp.int32), npass.astype(jnp.int32), n)


def _rope_table(n_ctx, n):
    t = np.arange(n)
    n_freq = MLA_ROPE // 4
    inv_freq = ROPE_THETA ** (-np.arange(n_freq, dtype=np.float64) / n_freq)
    ar = (t // GRID_W).astype(np.float64)[:, None] * inv_freq
    ac = (t % GRID_W).astype(np.float64)[:, None] * inv_freq
    cos = np.concatenate([np.cos(ar), np.cos(ar), np.cos(ac), np.cos(ac)], axis=1)
    sin = np.concatenate([-np.sin(ar), np.sin(ar), -np.sin(ac), np.sin(ac)], axis=1)
    tab = np.concatenate([cos, sin], axis=1)
    ident = np.concatenate([np.ones((n_ctx, MLA_ROPE)), np.zeros((n_ctx, MLA_ROPE))], axis=1)
    return jnp.asarray(np.concatenate([ident, tab], axis=0).astype(np.float32))


def _swap_pairs(w):
    q = MLA_ROPE // 4
    return jnp.concatenate([w[:, q:2 * q], w[:, :q], w[:, 3 * q:], w[:, 2 * q:3 * q]], axis=1)


def kernel(x, c, ctx, c_ctx, ada_w, ada_b, g_mix, g_ffn, g_final, w_in, g_qn, w_uq, g_kvn, w_ukv,
           na_rel_bias, w_out_attn, w_out_fourier, w_router, w_gate, w_up, w_down):
    _, n, d = x.shape
    n_ctx = ctx.shape[1]
    heads = d // 256
    q_rank = d // 4
    kv_rank = d // 8
    na_width = heads * NA_DIM
    mla_scale = (MLA_NOPE + MLA_ROPE) ** -0.5
    na_scale = NA_DIM ** -0.5
    xs = x[0]
    cx = ctx[0]

    cvecs = jnp.zeros((8, d), F32).at[0].set(c[0]).at[1].set(c_ctx)

    m0 = _ada_terms(cvecs, ada_w, ada_b, 0)
    sh_m, sc_m, g_m, sh_f, sc_f, g_f = [m0[0:1, k * d:(k + 1) * d] for k in range(6)]
    csh_m, csc_m = m0[1:2, 0:d], m0[1:2, d:2 * d]
    gm0 = g_mix[0][None]
    h_x = _modulate(xs, gm0, sh_m, sc_m)
    h_c = _modulate(cx, gm0, csh_m, csc_m)
    h_all = jnp.concatenate([h_c, h_x], axis=0)

    w0 = w_in[0]
    o1, o2, o3 = q_rank, q_rank + kv_rank, q_rank + kv_rank + MLA_ROPE
    w_kr = w0[:, o2:o3]
    w_a = jnp.concatenate([w0[:, :o2], w_kr, _swap_pairs(w_kr)], axis=1).astype(BF16)
    w_q = w0[:, o3:o3 + na_width].astype(BF16)
    w_kv = w0[:, o3 + na_width:].astype(BF16)
    ones = lambda k: jnp.ones((1, k), F32)
    a_all = _matmul(h_all, w_a, ones(w_a.shape[1]), F32)
    q_na = _matmul(h_x, w_q, jnp.full((1, na_width), na_scale * LOG2_E, F32), BF16)
    kv_na = _matmul(h_all, w_kv, ones(2 * na_width), BF16)

    cs = _rope_table(n_ctx, n)
    wq3 = w_uq[0].reshape(q_rank, heads, MLA_NOPE + MLA_ROPE)
    wq_rope = wq3[:, :, MLA_NOPE:]
    wq_swap = jnp.stack([_swap_pairs(wq_rope[:, h]) for h in range(heads)], axis=1)
    wq = jnp.concatenate([wq3, wq_swap], axis=2).transpose(1, 0, 2).astype(BF16)
    wkv = w_ukv[0].reshape(kv_rank, heads, MLA_NOPE + MLA_V).transpose(1, 0, 2).astype(BF16)
    tmp = _pick(math.gcd(n, n_ctx), (256, 128, 64, 8))
    qt_mla = _qprep(a_all, n_ctx // tmp, n, g_qn[0][None], wq, cs, heads, q_rank, mla_scale * LOG2_E, tmp)
    k_mla, vt_mla = _kvprep(a_all, g_kvn[0][None], wkv, cs, heads, q_rank, kv_rank, tmp)
    o_a = _mla_attention(qt_mla, k_mla, vt_mla)

    table = _na_bias_table(na_rel_bias[0])
    o_b = _na_attention(q_na, kv_na, _na_values_transposed(kv_na, heads), table, heads, n_ctx)
    x1 = _matmul_residual(o_a, o_b, w_out_attn[0].astype(BF16), xs, g_m)

    hf, aff = _modulate_router(x1, g_ffn[0][None], sh_f, sc_f, w_router[0])
    moe0 = _ec_moe(hf, aff, w_gate, w_up, w_down, 0)

    m1 = _ada_terms(cvecs, ada_w, ada_b, 1)
    sh_m1, sc_m1, g_m1, sh_f1, sc_f1, g_f1 = [m1[0:1, k * d:(k + 1) * d] for k in range(6)]
    x2, fr = _fourier_transform(x1, g_f, moe0, g_mix[1][None], sh_m1, sc_m1)
    x3 = _matmul_residual(fr, None, w_out_fourier[0].astype(BF16), x2, g_m1)
    hf1, aff1 = _modulate_router(x3, g_ffn[1][None], sh_f1, sc_f1, w_router[1])
    moe1 = _ec_moe(hf1, aff1, w_gate, w_up, w_down, 1)
    out = _final_norm(x3, g_f1, moe1, g_final[None])
    return out[None]
```

```python
import functools
import math

import jax
import jax.numpy as jnp
import numpy as np
from jax import lax
from jax.experimental import pallas as pl
from jax.experimental.pallas import tpu as pltpu

F32 = jnp.float32
BF16 = jnp.bfloat16

GRID_W = 64
NORM_EPS = 1e-6
MLA_NOPE = 128
MLA_ROPE = 64
MLA_V = 128
MLA_QK_PAD = 256
MLA_V_PAD = 144
ROPE_THETA = 10000.0
NA_DIM = 128
NA_WIN_H = 8
NA_WIN_W = 16
NA_ROW_BLOCK = 8
NA_KEY_ROWS = 16
FNET_GROUPS = 8
FFT_ROWS_PER_STEP = 4
N_EXPERTS = 16
EC_CAPACITY = 2
ROUTER_LANES = 128
COMBINE_BLOCK = 256
COMBINE_WIN = 64
COMBINE_ALIGN = 16
NEG_BIG = -1e30
LOG2_E = math.log2(math.e)

V7X_VMEM_LIMIT_BYTES = 56 * 1024 * 1024


def _cparams(sem, flags=None):
    return pltpu.CompilerParams(dimension_semantics=sem, vmem_limit_bytes=V7X_VMEM_LIMIT_BYTES, flags=flags)


def _pick(n, candidates):
    for c in candidates:
        if n % c == 0:
            return c
    return n


def _ada_kernel(c_ref, w_ref, b_ref, o_ref):
    c = c_ref[...]
    s = c / (1.0 + jnp.exp(-c))
    o_ref[...] = jnp.dot(s, w_ref[...], preferred_element_type=F32,
                         precision=lax.Precision.HIGHEST) + b_ref[...]


def _ada_terms(cvecs, w, b, layer):
    depth, d, n6 = w.shape
    tn = _pick(n6, (1024, 512, 256, 128))
    return pl.pallas_call(
        _ada_kernel,
        grid=(n6 // tn,),
        in_specs=[pl.BlockSpec((8, d), lambda j: (0, 0)),
                  pl.BlockSpec((None, d, tn), lambda j: (layer, 0, j)),
                  pl.BlockSpec((None, 1, tn), lambda j: (layer, 0, j))],
        out_specs=pl.BlockSpec((8, tn), lambda j: (0, j)),
        out_shape=jax.ShapeDtypeStruct((8, n6), F32),
        compiler_params=_cparams(("parallel",)),
        name="ada_terms",
    )(cvecs, w, b.reshape(depth, 1, n6))


def _modulated(x, g, shift, scale):
    ms = jnp.mean(x * x, axis=-1, keepdims=True)
    y = x * lax.rsqrt(ms + NORM_EPS) * g
    return y * (1.0 + scale) + shift


def _mod_kernel(x_ref, g_ref, sh_ref, sc_ref, h_ref):
    h_ref[...] = _modulated(x_ref[...], g_ref[...], sh_ref[...], sc_ref[...]).astype(h_ref.dtype)


def _modulate(x, g, shift, scale, out_dtype=BF16):
    n, d = x.shape
    tm = _pick(n, (512, 256, 128, 64, 8))
    row = pl.BlockSpec((tm, d), lambda i: (i, 0))
    vec = pl.BlockSpec((1, d), lambda i: (0, 0))
    return pl.pallas_call(
        _mod_kernel,
        grid=(n // tm,),
        in_specs=[row, vec, vec, vec],
        out_specs=row,
        out_shape=jax.ShapeDtypeStruct((n, d), out_dtype),
        compiler_params=_cparams(("parallel",)),
        name="modulate",
    )(x, g, shift, scale)


def _router_kernel(n_experts, x_ref, g_ref, sh_ref, sc_ref, whi_ref, wlo_ref, h_ref, aff_ref):
    h = _modulated(x_ref[...], g_ref[...], sh_ref[...], sc_ref[...])
    h_hi = h.astype(BF16)
    h_ref[...] = h_hi
    h_lo = (h - h_hi.astype(F32)).astype(BF16)
    logits = (jnp.dot(h_hi, whi_ref[...], preferred_element_type=F32)
              + jnp.dot(h_hi, wlo_ref[...], preferred_element_type=F32)
              + jnp.dot(h_lo, whi_ref[...], preferred_element_type=F32))
    lane = lax.broadcasted_iota(jnp.int32, logits.shape, 1)
    logits = jnp.where(lane < n_experts, logits, NEG_BIG)
    m = jnp.max(logits, axis=-1, keepdims=True)
    p = jnp.exp(logits - m)
    aff_ref[...] = p / jnp.sum(p, axis=-1, keepdims=True)


def _modulate_router(x, g, shift, scale, w_router):
    n, d = x.shape
    n_experts = w_router.shape[1]
    wr = jnp.zeros((d, ROUTER_LANES), F32).at[:, :n_experts].set(w_router)
    wr_hi = wr.astype(BF16)
    wr_lo = (wr - wr_hi.astype(F32)).astype(BF16)
    tm = _pick(n, (512, 256, 128, 64, 8))
    row = pl.BlockSpec((tm, d), lambda i: (i, 0))
    vec = pl.BlockSpec((1, d), lambda i: (0, 0))
    wspec = pl.BlockSpec((d, ROUTER_LANES), lambda i: (0, 0))
    return pl.pallas_call(
        functools.partial(_router_kernel, n_experts),
        grid=(n // tm,),
        in_specs=[row, vec, vec, vec, wspec, wspec],
        out_specs=[row, pl.BlockSpec((tm, ROUTER_LANES), lambda i: (i, 0))],
        out_shape=[jax.ShapeDtypeStruct((n, d), BF16),
                   jax.ShapeDtypeStruct((n, ROUTER_LANES), F32)],
        compiler_params=_cparams(("parallel",)),
        name="modulate_router",
    )(x, g, shift, scale, wr_hi, wr_lo)


def _resid_kernel(x_ref, gate_ref, y_ref, o_ref):
    o_ref[...] = x_ref[...] + gate_ref[...] * y_ref[...]


def _residual(x, gate, y):
    n, d = x.shape
    tm = _pick(n, (512, 256, 128, 64, 8))
    row = pl.BlockSpec((tm, d), lambda i: (i, 0))
    vec = pl.BlockSpec((1, d), lambda i: (0, 0))
    return pl.pallas_call(
        _resid_kernel,
        grid=(n // tm,),
        in_specs=[row, vec, row],
        out_specs=row,
        out_shape=jax.ShapeDtypeStruct((n, d), F32),
        compiler_params=_cparams(("parallel",)),
        name="residual",
    )(x, gate, y)


def _final_kernel(x_ref, gate_ref, y_ref, g_ref, o_ref):
    x = x_ref[...] + gate_ref[...] * y_ref[...]
    ms = jnp.mean(x * x, axis=-1, keepdims=True)
    o_ref[...] = x * lax.rsqrt(ms + NORM_EPS) * g_ref[...]


def _final_norm(x, gate, y, g):
    n, d = x.shape
    tm = _pick(n, (512, 256, 128, 64, 8))
    row = pl.BlockSpec((tm, d), lambda i: (i, 0))
    vec = pl.BlockSpec((1, d), lambda i: (0, 0))
    return pl.pallas_call(
        _final_kernel,
        grid=(n // tm,),
        in_specs=[row, vec, row, vec],
        out_specs=row,
        out_shape=jax.ShapeDtypeStruct((n, d), F32),
        compiler_params=_cparams(("parallel",)),
        name="final_norm",
    )(x, gate, y, g)


def _mm_kernel(a_ref, w_ref, cs_ref, o_ref):
    acc = jnp.dot(a_ref[...], w_ref[...], preferred_element_type=F32)
    o_ref[...] = (acc * cs_ref[...]).astype(o_ref.dtype)


def _mm_resid_kernel(a1_ref, a2_ref, w_ref, x_ref, gate_ref, o_ref):
    k1 = a1_ref.shape[1]
    acc = jnp.dot(a1_ref[...], w_ref[:k1, :], preferred_element_type=F32)
    acc = acc + jnp.dot(a2_ref[...], w_ref[k1:, :], preferred_element_type=F32)
    o_ref[...] = x_ref[...] + gate_ref[...] * acc


def _mm_resid_f32_kernel(a_ref, w_ref, x_ref, gate_ref, o_ref):
    acc = jnp.dot(a_ref[...].astype(BF16), w_ref[...], preferred_element_type=F32)
    o_ref[...] = x_ref[...] + gate_ref[...] * acc


def _mm_tiles(m, n):
    tm = _pick(m, (1280, 1024, 512, 256, 128, 64, 8))
    tn = n if n <= 1024 else _pick(n, (1024, 512, 256, 128))
    return tm, tn


def _matmul(a, w, col_scale, out_dtype):
    m, k = a.shape
    n = w.shape[1]
    tm, tn = _mm_tiles(m, n)
    return pl.pallas_call(
        _mm_kernel,
        grid=(m // tm, n // tn),
        in_specs=[pl.BlockSpec((tm, k), lambda i, j: (i, 0)),
                  pl.BlockSpec((k, tn), lambda i, j: (0, j)),
                  pl.BlockSpec((1, tn), lambda i, j: (0, j))],
        out_specs=pl.BlockSpec((tm, tn), lambda i, j: (i, j)),
        out_shape=jax.ShapeDtypeStruct((m, n), out_dtype),
        compiler_params=_cparams(("parallel", "parallel")),
        name="matmul",
    )(a, w, col_scale)


def _matmul_residual(a1, a2, w, x, gate):
    m = a1.shape[0]
    n = w.shape[1]
    k1 = w.shape[0] // 2
    tm, tn = _mm_tiles(m, n)
    a_spec = pl.BlockSpec((tm, k1), lambda i, j: (i, 0))
    operands, a_specs, body = (a1, a2), [a_spec, a_spec], _mm_resid_kernel
    if a2 is None:
        operands, a_specs, body = (a1,), [pl.BlockSpec((tm, 2 * k1), lambda i, j: (i, 0))], _mm_resid_f32_kernel
    return pl.pallas_call(
        body,
        grid=(m // tm, n // tn),
        in_specs=a_specs + [pl.BlockSpec((2 * k1, tn), lambda i, j: (0, j)),
                            pl.BlockSpec((tm, tn), lambda i, j: (i, j)),
                            pl.BlockSpec((1, tn), lambda i, j: (0, j))],
        out_specs=pl.BlockSpec((tm, tn), lambda i, j: (i, j)),
        out_shape=jax.ShapeDtypeStruct((m, n), F32),
        compiler_params=_cparams(("parallel", "parallel")),
        name="matmul_residual",
    )(*operands, w, x, gate)


def _rms(x, g):
    ms = jnp.mean(x * x, axis=-1, keepdims=True)
    return x * lax.rsqrt(ms + NORM_EPS) * g


def _rope_half(pair, cs):
    t = pair * cs
    t = t + pltpu.roll(t, MLA_ROPE, 1)
    lane = lax.broadcasted_iota(jnp.int32, t.shape, 1)
    return jnp.where(lane < MLA_ROPE, t, 0.0)


def _qprep_kernel(heads, q_rank, scale, a_ref, g_ref, w_ref, cs_ref, q_ref):
    cq = _rms(a_ref[:, :q_rank], g_ref[...]).astype(BF16)
    cs = cs_ref[...]
    for h in range(heads):
        qh = jnp.dot(cq, w_ref[h], preferred_element_type=F32)
        q = jnp.concatenate([qh[:, :MLA_NOPE], _rope_half(qh[:, MLA_NOPE:], cs)], axis=1) * scale
        q_ref[h] = q.T.astype(BF16)


def _kvprep_kernel(heads, q_rank, kv_rank, a_ref, g_ref, w_ref, cs_ref, k_ref, v_ref):
    ckv = _rms(a_ref[:, q_rank:q_rank + kv_rank], g_ref[...]).astype(BF16)
    kr = _rope_half(a_ref[:, q_rank + kv_rank:], cs_ref[...]).astype(BF16)
    row = lax.broadcasted_iota(jnp.int32, (MLA_V_PAD - MLA_V, a_ref.shape[0]), 0)
    ones_rows = jnp.where(row == 0, 1.0, 0.0)
    for h in range(heads):
        kvh = jnp.dot(ckv, w_ref[h], preferred_element_type=F32)
        k_ref[h, :, :MLA_NOPE] = kvh[:, :MLA_NOPE].astype(BF16)
        k_ref[h, :, MLA_NOPE:] = kr
        vt = kvh[:, MLA_NOPE:].T
        v_ref[h] = jnp.concatenate([vt, ones_rows], axis=0).astype(BF16)


def _qprep(a, row_off_blocks, n_rows, g_qn, wq, cs, heads, q_rank, scale, tm):
    aw = a.shape[1]
    return pl.pallas_call(
        functools.partial(_qprep_kernel, heads, q_rank, scale),
        grid=(n_rows // tm,),
        in_specs=[pl.BlockSpec((tm, aw), lambda i: (i + row_off_blocks, 0)),
                  pl.BlockSpec((1, q_rank), lambda i: (0, 0)),
                  pl.BlockSpec((heads, q_rank, MLA_QK_PAD), lambda i: (0, 0, 0)),
                  pl.BlockSpec((tm, 2 * MLA_ROPE), lambda i: (i + row_off_blocks, 0))],
        out_specs=pl.BlockSpec((heads, MLA_QK_PAD, tm), lambda i: (0, 0, i)),
        out_shape=jax.ShapeDtypeStruct((heads, MLA_QK_PAD, n_rows), BF16),
        compiler_params=_cparams(("parallel",)),
        name="mla_q_prep",
    )(a, g_qn, wq, cs)


def _kvprep(a, g_kvn, wkv, cs, heads, q_rank, kv_rank, tm):
    n_rows, aw = a.shape
    return pl.pallas_call(
        functools.partial(_kvprep_kernel, heads, q_rank, kv_rank),
        grid=(n_rows // tm,),
        in_specs=[pl.BlockSpec((tm, aw), lambda i: (i, 0)),
                  pl.BlockSpec((1, kv_rank), lambda i: (0, 0)),
                  pl.BlockSpec((heads, kv_rank, MLA_NOPE + MLA_V), lambda i: (0, 0, 0)),
                  pl.BlockSpec((tm, 2 * MLA_ROPE), lambda i: (i, 0))],
        out_specs=[pl.BlockSpec((heads, tm, MLA_QK_PAD), lambda i: (0, i, 0)),
                   pl.BlockSpec((heads, MLA_V_PAD, tm), lambda i: (0, 0, i))],
        out_shape=[jax.ShapeDtypeStruct((heads, n_rows, MLA_QK_PAD), BF16),
                   jax.ShapeDtypeStruct((heads, MLA_V_PAD, n_rows), BF16)],
        compiler_params=_cparams(("parallel",)),
        name="mla_kv_prep",
    )(a, g_kvn, wkv, cs)


def _dot_nt(a, b):
    return lax.dot_general(a, b, (((1,), (1,)), ((), ())), preferred_element_type=F32)


def _mla_kernel(tk, n_kv, n_sub, qt_ref, k_ref, vt_ref, o_ref, s_ref, p_ref, acc_ref):
    tq = qt_ref.shape[2] // n_sub
    n_chunks = n_kv // tk

    def trip(t, par, carry, do_a=True, do_b=True, do_c=True):
        ms, al_new, al_old = carry
        out_ms, out_al = ms, al_new
        if do_a:
            start = t * tk if isinstance(t, int) else pl.multiple_of(t * tk, tk)
            kc = k_ref[0, pl.ds(start, tk), :]
            out_ms, out_al = [], []
            for u in range(n_sub):
                s = jnp.dot(kc, qt_ref[0, :, u * tq:(u + 1) * tq], preferred_element_type=F32)
                s_ref[par, u] = s
                m_new = jnp.maximum(ms[u], jnp.max(s, axis=0, keepdims=True))
                out_al.append(jnp.exp2(ms[u] - m_new))
                out_ms.append(m_new)
            out_ms, out_al = tuple(out_ms), tuple(out_al)
        if do_b:
            for u in range(n_sub):
                p_ref[1 - par, u] = jnp.exp2((s_ref[1 - par, u] - ms[u]).astype(BF16))
        if do_c:
            start = (t - 2) * tk if isinstance(t, int) else pl.multiple_of((t - 2) * tk, tk)
            vc = vt_ref[0, :, pl.ds(start, tk)]
            for u in range(n_sub):
                acc_ref[u] = al_old[u] * acc_ref[u] + jnp.dot(vc, p_ref[par, u], preferred_element_type=F32)
        return out_ms, out_al, al_new

    acc_ref[...] = jnp.zeros_like(acc_ref)
    neg = tuple(jnp.full((1, tq), NEG_BIG, F32) for _ in range(n_sub))
    zero = tuple(jnp.zeros((1, tq), F32) for _ in range(n_sub))
    carry = (neg, zero, zero)
    carry = trip(0, 0, carry, do_b=False, do_c=False)
    if n_chunks > 1:
        carry = trip(1, 1, carry, do_c=False)
    n_steady = max(n_chunks - 2, 0)

    def double_trip(i, carry):
        t = 2 + 2 * i
        return trip(t + 1, 1, trip(t, 0, carry))

    carry = lax.fori_loop(0, n_steady // 2, double_trip, carry)
    if n_steady % 2 == 1:
        carry = trip(n_chunks - 1, (n_chunks - 1) % 2, carry)
    carry = trip(n_chunks, n_chunks % 2, carry, do_a=False, do_c=n_chunks > 1)
    if n_chunks > 1:
        carry = trip(n_chunks + 1, (n_chunks + 1) % 2, carry, do_a=False, do_b=False)
    else:
        carry = trip(2, 0, carry, do_a=False, do_b=False)
    for u in range(n_sub):
        o = acc_ref[u, :MLA_V, :] / acc_ref[u, MLA_V:MLA_V + 1, :]
        o_ref[u * tq:(u + 1) * tq, :] = o.T.astype(o_ref.dtype)


def _mla_attention(qt, k, vt):
    heads, _, n = qt.shape
    n_kv = k.shape[1]
    tq = _pick(n, (256, 128))
    n_sub = 4 if n % (4 * tq) == 0 else 1
    tk = _pick(n_kv, (1280, 768, 512, 256, 128))
    return pl.pallas_call(
        functools.partial(_mla_kernel, tk, n_kv, n_sub),
        grid=(heads, n // (tq * n_sub)),
        in_specs=[pl.BlockSpec((1, MLA_QK_PAD, tq * n_sub), lambda h, i: (h, 0, i)),
                  pl.BlockSpec((1, n_kv, MLA_QK_PAD), lambda h, i: (h, 0, 0)),
                  pl.BlockSpec((1, MLA_V_PAD, n_kv), lambda h, i: (h, 0, 0))],
        out_specs=pl.BlockSpec((tq * n_sub, MLA_V), lambda h, i: (i, h)),
        out_shape=jax.ShapeDtypeStruct((n, heads * MLA_V), BF16),
        scratch_shapes=[pltpu.VMEM((2, n_sub, tk, tq), F32),
                        pltpu.VMEM((2, n_sub, tk, tq), BF16),
                        pltpu.VMEM((n_sub, MLA_V_PAD, tq), F32)],
        compiler_params=_cparams(("parallel", "parallel")),
        name="mla_attention",
    )(qt, k, vt)


def _na_kernel(n_ctx, rows_total, win_h, key_rows, q_ref, k_ref, vt_ref, t_ref, o_ref,
               bm_ref, s_ref, p_ref, m_ref):
    hd = pl.program_id(0)
    b = pl.program_id(1)
    nb = rows_total // NA_ROW_BLOCK
    n_loc = key_rows * GRID_W
    par = b % 2

    def window_start(blk):
        base = jnp.clip(blk * NA_ROW_BLOCK - NA_WIN_H // 2, 0, rows_total - key_rows)
        return pl.multiple_of(n_ctx + base * GRID_W, 128)

    @pl.when((hd == 0) & (b == 0))
    def _():
        s_ref[...] = jnp.zeros_like(s_ref)
        p_ref[...] = jnp.ones_like(p_ref)
        m_ref[...] = jnp.zeros_like(m_ref)

    @pl.when(b == 0)
    def _():
        for v, blk in enumerate((0, min(1, nb - 1), nb - 1)):
            r0 = blk * NA_ROW_BLOCK
            base = min(max(r0 - NA_WIN_H // 2, 0), rows_total - key_rows)
            for i in range(NA_ROW_BLOCK):
                rs = min(max(r0 + i - win_h // 2, 0), rows_total - win_h)
                for j in range(key_rows):
                    kr = base + j
                    tile = (slice(j * GRID_W, (j + 1) * GRID_W), slice(i * GRID_W, (i + 1) * GRID_W))
                    if rs <= kr < rs + win_h:
                        bm_ref[(v,) + tile] = t_ref[0, kr - (r0 + i) + NA_WIN_H - 1]
                    else:
                        bm_ref[(v,) + tile] = jnp.full((GRID_W, GRID_W), NEG_BIG, F32)

    def stages(par):
        ta = jnp.minimum(b, nb - 1)
        variant = jnp.where(ta == 0, 0, jnp.where(ta == nb - 1, 2, 1))
        q = q_ref[...]
        s_ctx = _dot_nt(k_ref[0:n_ctx, :], q)
        s_loc = _dot_nt(k_ref[pl.ds(window_start(ta), n_loc), :], q) + bm_ref[variant]
        s_ref[par, 0:n_ctx, :] = s_ctx
        s_ref[par, n_ctx:, :] = s_loc
        m_ref[par] = jnp.maximum(jnp.max(s_ctx, axis=0, keepdims=True), jnp.max(s_loc, axis=0, keepdims=True))

        p_ref[1 - par] = jnp.exp2((s_ref[1 - par] - m_ref[1 - par]).astype(BF16))

        tc = jnp.clip(b - 2, 0, nb - 1)
        acc = jnp.dot(vt_ref[0, :, 0:n_ctx], p_ref[par, 0:n_ctx, :], preferred_element_type=F32)
        acc = acc + jnp.dot(vt_ref[0, :, pl.ds(window_start(tc), n_loc)], p_ref[par, n_ctx:, :],
                            preferred_element_type=F32)
        o = acc[:NA_DIM] / acc[NA_DIM:NA_DIM + 1]
        o_ref[...] = o.T.astype(o_ref.dtype)

    for static_par in (0, 1):
        pl.when(par == static_par)(functools.partial(stages, static_par))


def _na_bias_table(rel_bias):
    cq = jnp.arange(GRID_W)
    ck = jnp.arange(GRID_W)
    col_start = jnp.clip(cq - NA_WIN_W // 2, 0, GRID_W - NA_WIN_W)
    valid = (ck[:, None] >= col_start[None, :]) & (ck[:, None] < col_start[None, :] + NA_WIN_W)
    co = jnp.clip(ck[:, None] - cq[None, :] + NA_WIN_W - 1, 0, 2 * NA_WIN_W - 2)
    onehot = (co[None] == jnp.arange(2 * NA_WIN_W - 1)[:, None, None]).astype(F32)
    t = jnp.einsum('hrc,ckq->hrkq', rel_bias, onehot, precision=lax.Precision.HIGHEST) * LOG2_E
    return jnp.where(valid[None, None], t, NEG_BIG)


def _vt_prep_kernel(heads, x_ref, o_ref):
    row = lax.broadcasted_iota(jnp.int32, (MLA_V_PAD - NA_DIM, x_ref.shape[0]), 0)
    ones_rows = jnp.where(row == 0, 1.0, 0.0)
    for h in range(heads):
        vt = x_ref[:, h * NA_DIM:(h + 1) * NA_DIM].astype(F32).T
        o_ref[h] = jnp.concatenate([vt, ones_rows], axis=0).astype(BF16)


def _na_values_transposed(kv, heads):
    n_kv = kv.shape[0]
    tm = _pick(n_kv, (256, 128))
    return pl.pallas_call(
        functools.partial(_vt_prep_kernel, heads),
        grid=(n_kv // tm,),
        in_specs=[pl.BlockSpec((tm, heads * NA_DIM), lambda i: (i, 1))],
        out_specs=pl.BlockSpec((heads, MLA_V_PAD, tm), lambda i: (0, 0, i)),
        out_shape=jax.ShapeDtypeStruct((heads, MLA_V_PAD, n_kv), BF16),
        compiler_params=_cparams(("parallel",)),
        name="na_v_prep",
    )(kv)


def _na_attention(q, kv, vt, table, heads, n_ctx):
    n = q.shape[0]
    n_kv = kv.shape[0]
    rows_total = n // GRID_W
    win_h = min(NA_WIN_H, rows_total)
    key_rows = min(NA_KEY_ROWS, rows_total)
    tq = NA_ROW_BLOCK * GRID_W
    nb = n // tq
    n_loc = key_rows * GRID_W
    n_off = table.shape[1]
    return pl.pallas_call(
        functools.partial(_na_kernel, n_ctx, rows_total, win_h, key_rows),
        grid=(heads, nb + 2),
        in_specs=[pl.BlockSpec((tq, NA_DIM), lambda h, i: (jnp.minimum(i, nb - 1), h)),
                  pl.BlockSpec((n_kv, NA_DIM), lambda h, i: (0, h)),
                  pl.BlockSpec((1, MLA_V_PAD, n_kv), lambda h, i: (h, 0, 0)),
                  pl.BlockSpec((1, n_off, GRID_W, GRID_W), lambda h, i: (h, 0, 0, 0))],
        out_specs=pl.BlockSpec((tq, NA_DIM), lambda h, i: (jnp.clip(i - 2, 0, nb - 1), h)),
        out_shape=jax.ShapeDtypeStruct((n, heads * NA_DIM), BF16),
        scratch_shapes=[pltpu.VMEM((3, n_loc, tq), F32),
                        pltpu.VMEM((2, n_ctx + n_loc, tq), F32),
                        pltpu.VMEM((2, n_ctx + n_loc, tq), BF16),
                        pltpu.VMEM((2, 1, tq), F32)],
        compiler_params=_cparams(("arbitrary", "arbitrary")),
        name="na_attention",
    )(q, kv, vt, table)


def _fft_a_kernel(groups, x_ref, gate_ref, y_ref, g_ref, sh_ref, sc_ref, w_ref, xo_ref, z_ref):
    x = x_ref[...] + gate_ref[...] * y_ref[...]
    xo_ref[...] = x
    h = _modulated(x, g_ref[...], sh_ref[...], sc_ref[...]).astype(BF16)
    d = h.shape[1]
    dg = d // groups
    for g in range(groups):
        z = jnp.dot(h[:, g * dg:(g + 1) * dg], w_ref[...], preferred_element_type=F32)
        z_ref[:, g * dg:(g + 1) * dg] = z[:, :dg]
        z_ref[:, d + g * dg:d + (g + 1) * dg] = z[:, dg:]


def _strided_fetch(src_hbm, buf, sem, step, slot, rows):
    return [pltpu.make_async_copy(src_hbm.at[:, step * rows + r, :], buf.at[slot, r], sem.at[slot, r])
            for r in range(rows)]


def _prefetch_next(src_hbm, buf, sem, rows):
    j = pl.program_id(0)
    slot = j % 2

    @pl.when(j == 0)
    def _():
        for r, c in enumerate(_strided_fetch(src_hbm, buf, sem, 0, 0, rows)):
            c.start(priority=r % 2)

    @pl.when(j + 1 < pl.num_programs(0))
    def _():
        for r, c in enumerate(_strided_fetch(src_hbm, buf, sem, j + 1, 1 - slot, rows)):
            c.start(priority=r % 2)

    return j, slot


def _fft_b_kernel(z_hbm, w_ref, y_ref, buf, sem):
    rows, n1 = y_ref.shape[0], y_ref.shape[1]
    d = y_ref.shape[2] // 2
    j, slot = _prefetch_next(z_hbm, buf, sem, rows)
    waits = _strided_fetch(z_hbm, buf, sem, j, slot, rows)
    for r in range(rows):
        waits[r].wait()
        z = jnp.concatenate([buf[slot, r, :, :d], buf[slot, r, :, d:]], axis=0).astype(BF16)
        y = jnp.dot(w_ref[...], z, preferred_element_type=F32)
        y_ref[r, :, :d] = y[:n1]
        y_ref[r, :, d:] = y[n1:]


def _fft_c_kernel(y_hbm, w_ref, f_hbm, buf, obuf, sem, osem):
    rows = buf.shape[1]
    d = obuf.shape[2]
    j, slot = _prefetch_next(y_hbm, buf, sem, rows)
    waits = _strided_fetch(y_hbm, buf, sem, j, slot, rows)
    puts = [pltpu.make_async_copy(obuf.at[r], f_hbm.at[:, j * rows + r, :], osem.at[r]) for r in range(rows)]
    for r in range(rows):
        waits[r].wait()
        y = jnp.concatenate([buf[slot, r, :, :d], buf[slot, r, :, d:]], axis=0).astype(BF16)
        obuf[r] = jnp.dot(w_ref[r], y, preferred_element_type=F32)
        puts[r].start()
    for r in range(rows):
        puts[r].wait()


def _dft_tables(n, dg):
    lg = int(math.log2(n))
    n1 = 1 << ((lg + 1) // 2)
    n2 = n // n1
    scale_a = 1.0 / math.sqrt(dg)
    sb = 1 << (int(math.log2(n1)) // 2)
    scale_b = 1.0 / sb
    scale_c = 1.0 / (math.sqrt(n) / sb)
    ch = np.arange(dg)
    ang = (2.0 * np.pi / dg) * ((ch[:, None] * ch[None, :]) % dg)
    wa = np.concatenate([np.cos(ang), -np.sin(ang)], axis=1) * scale_a
    a = np.arange(n1)
    ang = (2.0 * np.pi / n1) * ((a[:, None] * a[None, :]) % n1)
    c, s = np.cos(ang), np.sin(ang)
    wb = np.concatenate([np.concatenate([c, s], axis=1), np.concatenate([-s, c], axis=1)], axis=0) * scale_b
    k1 = np.arange(n1)[:, None, None]
    k2 = np.arange(n2)[None, :, None]
    b = np.arange(n2)[None, None, :]
    ang = (2.0 * np.pi / n) * ((b * (k1 + n1 * k2)) % n)
    wc = np.concatenate([np.cos(ang), np.sin(ang)], axis=2) * scale_c
    as_bf16 = lambda t: jnp.asarray(t.astype(np.float32)).astype(BF16)
    return n1, n2, as_bf16(wa), as_bf16(wb), as_bf16(wc)


def _fourier_transform(x, gate, y, g, shift, scale):
    n, d = x.shape
    dg = d // FNET_GROUPS
    n1, n2, wa, wb, wc = _dft_tables(n, dg)
    tm = _pick(n, (512, 256, 128, 64, 8))
    row = pl.BlockSpec((tm, d), lambda i: (i, 0))
    vec = pl.BlockSpec((1, d), lambda i: (0, 0))
    x_new, z = pl.pallas_call(
        functools.partial(_fft_a_kernel, FNET_GROUPS),
        grid=(n // tm,),
        in_specs=[row, vec, row, vec, vec, vec, pl.BlockSpec((dg, 2 * dg), lambda i: (0, 0))],
        out_specs=[row, pl.BlockSpec((tm, 2 * d), lambda i: (i, 0))],
        out_shape=[jax.ShapeDtypeStruct((n, d), F32),
                   jax.ShapeDtypeStruct((n, 2 * d), F32)],
        compiler_params=_cparams(("parallel",)),
        name="fft_channels",
    )(x, gate, y, g, shift, scale, wa)

    gb = _pick(n2, (FFT_ROWS_PER_STEP,))
    yw = pl.pallas_call(
        _fft_b_kernel,
        grid=(n2 // gb,),
        in_specs=[pl.BlockSpec(memory_space=pl.ANY), pl.BlockSpec((2 * n1, 2 * n1), lambda j: (0, 0))],
        out_specs=pl.BlockSpec((gb, n1, 2 * d), lambda j: (j, 0, 0)),
        out_shape=jax.ShapeDtypeStruct((n2, n1, 2 * d), F32),
        scratch_shapes=[pltpu.VMEM((2, gb, n1, 2 * d), F32), pltpu.SemaphoreType.DMA((2, gb))],
        compiler_params=_cparams(("arbitrary",)),
        name="fft_pos_outer",
    )(z.reshape(n1, n2, 2 * d), wb)

    gc = _pick(n1, (FFT_ROWS_PER_STEP,))
    f = pl.pallas_call(
        _fft_c_kernel,
        grid=(n1 // gc,),
        in_specs=[pl.BlockSpec(memory_space=pl.ANY), pl.BlockSpec((gc, n2, 2 * n2), lambda k: (k, 0, 0))],
        out_specs=pl.BlockSpec(memory_space=pl.ANY),
        out_shape=jax.ShapeDtypeStruct((n2, n1, d), F32),
        scratch_shapes=[pltpu.VMEM((2, gc, n2, 2 * d), F32), pltpu.VMEM((gc, n2, d), F32),
                        pltpu.SemaphoreType.DMA((2, gc)), pltpu.SemaphoreType.DMA((gc,))],
        compiler_params=_cparams(("arbitrary",)),
        name="fft_pos_inner",
    )(yw, wc)
    return x_new, f.reshape(n, d)


def _expert_kernel(x_ref, wg_ref, wu_ref, wd_ref, gate_ref, o_ref, acc_ref):
    f = pl.program_id(2)

    @pl.when(f == 0)
    def _():
        acc_ref[...] = jnp.zeros_like(acc_ref)

    x = x_ref[0]
    a = jnp.dot(x, wg_ref[...].astype(BF16), preferred_element_type=F32)
    u = jnp.dot(x, wu_ref[...].astype(BF16), preferred_element_type=F32)
    hmid = (a / (1.0 + jnp.exp(-a)) * u).astype(BF16)
    acc_ref[...] += jnp.dot(hmid, wd_ref[...].astype(BF16), preferred_element_type=F32)

    @pl.when(f == pl.num_programs(2) - 1)
    def _():
        o_ref[0] = (acc_ref[...] * gate_ref[0]).astype(o_ref.dtype)


def _expert_ffn(xs, w_gate, w_up, w_down, gates, layer):
    e, cap, d = xs.shape
    ff = w_gate.shape[3]
    tm = _pick(cap, (1024, 512, 256, 128, 64, 8))
    tf = _pick(ff, (256, 128))
    return pl.pallas_call(
        _expert_kernel,
        grid=(e, cap // tm, ff // tf),
        in_specs=[pl.BlockSpec((1, tm, d), lambda ei, i, f: (ei, i, 0)),
                  pl.BlockSpec((None, None, d, tf), lambda ei, i, f: (layer, ei, 0, f)),
                  pl.BlockSpec((None, None, d, tf), lambda ei, i, f: (layer, ei, 0, f)),
                  pl.BlockSpec((None, None, tf, d), lambda ei, i, f: (layer, ei, f, 0)),
                  pl.BlockSpec((1, tm, 1), lambda ei, i, f: (ei, i, 0))],
        out_specs=pl.BlockSpec((1, tm, d), lambda ei, i, f: (ei, i, 0)),
        out_shape=jax.ShapeDtypeStruct((e, cap, d), BF16),
        scratch_shapes=[pltpu.VMEM((tm, d), F32)],
        compiler_params=_cparams(("parallel", "parallel", "arbitrary")),
        name="expert_ffn",
    )(xs, w_gate, w_up, w_down, gates)


def _combine_kernel(n_exp, cap, w0_ref, npass_ref, slot_ref, w0v_ref, y_hbm, r_ref, o_ref, buf_ref, acc_ref, sem):
    b = pl.program_id(0)
    nb = pl.num_programs(0)
    win = COMBINE_WIN
    par = b % 2
    slot = slot_ref[...]
    w0v = w0v_ref[...]
    jio = (lax.broadcasted_iota(jnp.int32, (1, n_exp * win), 1) % win).astype(F32)

    def window_copies(blk, p, buf_slot):
        copies = []
        for e in range(n_exp):
            st = pl.multiple_of(jnp.minimum(w0_ref[blk * n_exp + e] + p * win, cap - win), COMBINE_ALIGN)
            copies.append(pltpu.make_async_copy(y_hbm.at[e, pl.ds(st, win), :],
                                                buf_ref.at[buf_slot, pl.ds(e * win, win), :],
                                                sem.at[buf_slot, e]))
        return copies

    def onehot_of(p):
        lo = w0v + p * win
        st_vec = jnp.minimum(lo, cap - win)
        valid = (slot >= lo) & (slot < lo + win)
        rel = jnp.where(valid, slot - st_vec, -1).astype(F32).astype(BF16)
        relx = jnp.dot(rel, r_ref[...], preferred_element_type=F32)
        return jnp.where(relx == jio, 1.0, 0.0).astype(BF16)

    @pl.when(b == 0)
    def _():
        for c in window_copies(0, 0, 0):
            c.start()

    @pl.when(b + 1 < nb)
    def _():
        for c in window_copies(b + 1, 0, 1 - par):
            c.start()

    onehot = onehot_of(0)
    for c in window_copies(b, 0, par):
        c.wait()
    acc_ref[...] = jnp.dot(onehot, buf_ref[par], preferred_element_type=F32)

    def extra_pass(p, carry):
        copies = window_copies(b, p, par)
        for c in copies:
            c.start()
        onehot = onehot_of(p)
        for c in copies:
            c.wait()
        acc_ref[...] += jnp.dot(onehot, buf_ref[par], preferred_element_type=F32)
        return carry

    lax.fori_loop(1, npass_ref[b], extra_pass, 0)
    o_ref[...] = acc_ref[...].astype(o_ref.dtype)


def _combine(y, slot, w0, npass, n):
    n_exp, cap, d = y.shape
    tb = _pick(n, (COMBINE_BLOCK,))
    nb = n // tb
    win = COMBINE_WIN
    rows = lax.broadcasted_iota(jnp.int32, (ROUTER_LANES, n_exp * win), 0)
    cols = lax.broadcasted_iota(jnp.int32, (ROUTER_LANES, n_exp * win), 1)
    spread = (cols // win == rows).astype(BF16)
    w0v = jnp.zeros((nb, 1, ROUTER_LANES), jnp.int32).at[:, 0, :n_exp].set(w0)
    grid_spec = pltpu.PrefetchScalarGridSpec(
        num_scalar_prefetch=2,
        grid=(nb,),
        in_specs=[pl.BlockSpec((tb, ROUTER_LANES), lambda b, *_: (b, 0)),
                  pl.BlockSpec((None, 1, ROUTER_LANES), lambda b, *_: (b, 0, 0)),
                  pl.BlockSpec(memory_space=pl.ANY),
                  pl.BlockSpec((ROUTER_LANES, n_exp * win), lambda b, *_: (0, 0))],
        out_specs=pl.BlockSpec((tb, d), lambda b, *_: (b, 0)),
        scratch_shapes=[pltpu.VMEM((2, n_exp * win, d), BF16),
                        pltpu.VMEM((tb, d), F32),
                        pltpu.SemaphoreType.DMA((2, n_exp))])
    return pl.pallas_call(
        functools.partial(_combine_kernel, n_exp, cap),
        grid_spec=grid_spec,
        out_shape=jax.ShapeDtypeStruct((n, d), BF16),
        compiler_params=_cparams(("arbitrary",)),
        name="moe_combine",
    )(w0.reshape(-1), npass, slot, w0v, y, spread)


def _ec_moe(hf, aff, w_gate, w_up, w_down, layer):
    n, d = hf.shape
    e = w_gate.shape[1]
    cap = EC_CAPACITY * n // e
    aff_e = aff[:, :e]
    gates, idx = lax.top_k(aff_e.T, cap)
    idx_s, gates_s = lax.sort((idx, gates), dimension=1, num_keys=1)
    xs = jnp.take(hf, idx_s.reshape(-1), axis=0).reshape(e, cap, d)
    y = _expert_ffn(xs, w_gate, w_up, w_down, gates_s[..., None], layer)

    thr = gates[:, cap - 1][None, :]
    gt = aff_e > thr
    eq = aff_e == thr
    need = cap - jnp.sum(gt, axis=0, keepdims=True)
    eq_i = eq.astype(jnp.int32)
    sel = gt | (eq & (jnp.cumsum(eq_i, axis=0) - eq_i < need))
    sel_i = sel.astype(jnp.int32)
    pos = jnp.cumsum(sel_i, axis=0) - sel_i
    slot = jnp.full((n, ROUTER_LANES), -1, jnp.int32).at[:, :e].set(jnp.where(sel, pos, -1))
    tb = _pick(n, (COMBINE_BLOCK,))
    off = pos[::tb]
    cnt = jnp.sum(sel_i.reshape(n // tb, tb, e), axis=1)
    w0 = jnp.minimum(off // COMBINE_ALIGN * COMBINE_ALIGN, cap - COMBINE_WIN)
    npass = jnp.max(jnp.where(cnt > 0, (off + cnt - w0 + COMBINE_WIN - 1) // COMBINE_WIN, 0), axis=1)
    return _combine(y, slot, w0.astype(jnp.int32), npass.astype(jnp.int32), n)


def _rope_table(n_ctx, n):
    t = np.arange(n)
    n_freq = MLA_ROPE // 4
    inv_freq = ROPE_THETA ** (-np.arange(n_freq, dtype=np.float64) / n_freq)
    ar = (t // GRID_W).astype(np.float64)[:, None] * inv_freq
    ac = (t % GRID_W).astype(np.float64)[:, None] * inv_freq
    cos = np.concatenate([np.cos(ar), np.cos(ar), np.cos(ac), np.cos(ac)], axis=1)
    sin = np.concatenate([-np.sin(ar), np.sin(ar), -np.sin(ac), np.sin(ac)], axis=1)
    tab = np.concatenate([cos, sin], axis=1)
    ident = np.concatenate([np.ones((n_ctx, MLA_ROPE)), np.zeros((n_ctx, MLA_ROPE))], axis=1)
    return jnp.asarray(np.concatenate([ident, tab], axis=0).astype(np.float32))


def _swap_pairs(w):
    q = MLA_ROPE // 4
    return jnp.concatenate([w[:, q:2 * q], w[:, :q], w[:, 3 * q:], w[:, 2 * q:3 * q]], axis=1)


def kernel(x, c, ctx, c_ctx, ada_w, ada_b, g_mix, g_ffn, g_final, w_in, g_qn, w_uq, g_kvn, w_ukv,
           na_rel_bias, w_out_attn, w_out_fourier, w_router, w_gate, w_up, w_down):
    _, n, d = x.shape
    n_ctx = ctx.shape[1]
    heads = d // 256
    q_rank = d // 4
    kv_rank = d // 8
    na_width = heads * NA_DIM
    mla_scale = (MLA_NOPE + MLA_ROPE) ** -0.5
    na_scale = NA_DIM ** -0.5
    xs = x[0]
    cx = ctx[0]

    cvecs = jnp.zeros((8, d), F32).at[0].set(c[0]).at[1].set(c_ctx)

    m0 = _ada_terms(cvecs, ada_w, ada_b, 0)
    sh_m, sc_m, g_m, sh_f, sc_f, g_f = [m0[0:1, k * d:(k + 1) * d] for k in range(6)]
    csh_m, csc_m = m0[1:2, 0:d], m0[1:2, d:2 * d]
    gm0 = g_mix[0][None]
    h_x = _modulate(xs, gm0, sh_m, sc_m)
    h_c = _modulate(cx, gm0, csh_m, csc_m)
    h_all = jnp.concatenate([h_c, h_x], axis=0)

    w0 = w_in[0]
    o1, o2, o3 = q_rank, q_rank + kv_rank, q_rank + kv_rank + MLA_ROPE
    w_kr = w0[:, o2:o3]
    w_a = jnp.concatenate([w0[:, :o2], w_kr, _swap_pairs(w_kr)], axis=1).astype(BF16)
    w_q = w0[:, o3:o3 + na_width].astype(BF16)
    w_kv = w0[:, o3 + na_width:].astype(BF16)
    ones = lambda k: jnp.ones((1, k), F32)
    a_all = _matmul(h_all, w_a, ones(w_a.shape[1]), F32)
    q_na = _matmul(h_x, w_q, jnp.full((1, na_width), na_scale * LOG2_E, F32), BF16)
    kv_na = _matmul(h_all, w_kv, ones(2 * na_width), BF16)

    cs = _rope_table(n_ctx, n)
    wq3 = w_uq[0].reshape(q_rank, heads, MLA_NOPE + MLA_ROPE)
    wq_rope = wq3[:, :, MLA_NOPE:]
    wq_swap = jnp.stack([_swap_pairs(wq_rope[:, h]) for h in range(heads)], axis=1)
    wq = jnp.concatenate([wq3, wq_swap], axis=2).transpose(1, 0, 2).astype(BF16)
    wkv = w_ukv[0].reshape(kv_rank, heads, MLA_NOPE + MLA_V).transpose(1, 0, 2).astype(BF16)
    tmp = _pick(math.gcd(n, n_ctx), (256, 128, 64, 8))
    qt_mla = _qprep(a_all, n_ctx // tmp, n, g_qn[0][None], wq, cs, heads, q_rank, mla_scale * LOG2_E, tmp)
    k_mla, vt_mla = _kvprep(a_all, g_kvn[0][None], wkv, cs, heads, q_rank, kv_rank, tmp)
    o_a = _mla_attention(qt_mla, k_mla, vt_mla)

    table = _na_bias_table(na_rel_bias[0])
    o_b = _na_attention(q_na, kv_na, _na_values_transposed(kv_na, heads), table, heads, n_ctx)
    x1 = _matmul_residual(o_a, o_b, w_out_attn[0].astype(BF16), xs, g_m)

    hf, aff = _modulate_router(x1, g_ffn[0][None], sh_f, sc_f, w_router[0])
    moe0 = _ec_moe(hf, aff, w_gate, w_up, w_down, 0)

    m1 = _ada_terms(cvecs, ada_w, ada_b, 1)
    sh_m1, sc_m1, g_m1, sh_f1, sc_f1, g_f1 = [m1[0:1, k * d:(k + 1) * d] for k in range(6)]
    x2, fr = _fourier_transform(x1, g_f, moe0, g_mix[1][None], sh_m1, sc_m1)
    x3 = _matmul_residual(fr, None, w_out_fourier[0].astype(BF16), x2, g_m1)
    hf1, aff1 = _modulate_router(x3, g_ffn[1][None], sh_f1, sc_f1, w_router[1])
    moe1 = _ec_moe(hf1, aff1, w_gate, w_up, w_down, 1)
    out = _final_norm(x3, g_f1, moe1, g_final[None])
    return out[None]
```
